```python
import math
import jax
import jax.numpy as jnp
from jax import lax
import numpy as np

D_MODEL = 1024
BATCH = 32
SEQ = 2048
DEPTH = 2
DEC_BATCH = 128
DEC_SEQ = 4
PAST_LEN = 16384
PAGE_SIZE = 128

N_EVEN = (DEPTH + 1) // 2
N_ODD = DEPTH // 2
H_A = 8
DH_NOPE = 64
DH_ROPE = 32
DH_V = 64
R_Q = 256
R_KV = 128
H_B = 8
DH_B = 64
DIL_GROUPS = ((128, 1), (512, 4), (2048, 16))
N_GROUPS = 3
H_C = 4
DH_C = 64
N_MEM = 256
H_M = 4
DH_M = 64
D_FF = 2816
ROPE_THETA = 10000.0
RMS_EPS = 1e-6
Q_BLOCK = 128
NEG_INF = -1e30
MLA_SCALE = 1.0 / math.sqrt(DH_NOPE + DH_ROPE)
SB_SCALE = 1.0 / math.sqrt(DH_B)
DIL_SCALE = 1.0 / math.sqrt(DH_C)
MEM_SCALE = 1.0 / math.sqrt(DH_M)
EV_WIDTHS = (R_Q, R_KV, DH_ROPE, H_B * DH_B, H_B * DH_B, H_B * DH_B)
EV_IN = R_Q + R_KV + DH_ROPE + 3 * H_B * DH_B
EV_OUT = H_A * DH_V + H_B * DH_B
OD_IN = N_GROUPS * 3 * H_C * DH_C
OD_OUT = H_C * DH_C

kernel_name = 'hybrid_mla_stickbreak_dilated_decoder_step'


def rmsnorm(x, g):
    xf = x.astype(jnp.float32)
    y = xf * lax.rsqrt(jnp.mean(xf * xf, axis=-1, keepdims=True) + RMS_EPS)
    return (y * g.astype(jnp.float32)).astype(x.dtype)


def rotary(x, pos):
    half = x.shape[-1] // 2
    inv_freq = ROPE_THETA ** (-jnp.arange(half, dtype=jnp.float32) / half)
    ang = pos.astype(jnp.float32)[:, None] * inv_freq[None, :]
    cos = jnp.cos(ang)[:, None, :]
    sin = jnp.sin(ang)[:, None, :]
    xf = x.astype(jnp.float32)
    x1, x2 = xf[..., :half], xf[..., half:]
    return jnp.concatenate([x1 * cos - x2 * sin, x2 * cos + x1 * sin], axis=-1).astype(x.dtype)


def swiglu_half(x, g, w_gu, w_down):
    h = rmsnorm(x, g) @ w_gu
    gate, up = jnp.split(h, 2, axis=-1)
    return 0.5 * ((jax.nn.silu(gate) * up) @ w_down)


def map_query_blocks(fn, qs, q_pos):
    B, S = qs[0].shape[:2]
    nb = S // Q_BLOCK
    qb = tuple(jnp.moveaxis(a.reshape(B, nb, Q_BLOCK, *a.shape[2:]), 1, 0) for a in qs)
    pb = q_pos.reshape(nb, Q_BLOCK)
    out = lax.map(lambda args: fn(*args), (*qb, pb))
    return jnp.moveaxis(out, 0, 1).reshape(B, S, *out.shape[3:])


def gather_pages(pool, layer_idx, page_table):
    rows = pool[layer_idx, page_table]
    return rows.reshape(page_table.shape[0], -1, *pool.shape[3:])


def even_project(h, pos, w_in, g_q, g_kv, w_uq, w_uk):
    B, T, _ = h.shape
    cuts = [int(c) for c in np.cumsum(EV_WIDTHS)[:-1]]
    c_q, c_kv, k_pe, q_b, k_b, v_b = jnp.split(h @ w_in, cuts, axis=-1)
    c_q = rmsnorm(c_q, g_q)
    c_kv = rmsnorm(c_kv, g_kv)
    q = jnp.einsum('btr,rhe->bthe', c_q, w_uq)
    q_pe = rotary(q[..., DH_NOPE:], pos)
    q_lat = jnp.einsum('bthn,rhn->bthr', q[..., :DH_NOPE], w_uk)
    k_pe = rotary(k_pe[:, :, None, :], pos)[:, :, 0]
    heads = lambda a: a.reshape(B, T, H_B, DH_B)
    return q_lat, q_pe, c_kv, k_pe, heads(q_b), heads(k_b), heads(v_b)


def mla_attend(q_lat, q_pe, c_kv, k_pe, q_pos, k_pos):
    s = (jnp.einsum('bthr,bsr->bhts', q_lat, c_kv).astype(jnp.float32)
         + jnp.einsum('bthe,bse->bhts', q_pe, k_pe).astype(jnp.float32)) * MLA_SCALE
    s = jnp.where(k_pos[None, :] <= q_pos[:, None], s, NEG_INF)
    p = jax.nn.softmax(s, axis=-1)
    return jnp.einsum('bhts,bsr->bthr', p.astype(c_kv.dtype), c_kv)


def sb_attend(q, k, v, q_pos, k_pos):
    z = jnp.einsum('bthd,bshd->bhts', q, k).astype(jnp.float32) * SB_SCALE
    valid = k_pos[None, :] < q_pos[:, None]
    log_keep = jnp.where(valid, jax.nn.log_sigmoid(-z), 0.0)
    later = lax.cumsum(log_keep, axis=3, reverse=True) - log_keep
    w = jnp.where(valid, jnp.exp(jax.nn.log_sigmoid(z) + later), 0.0)
    return jnp.einsum('bhts,bshd->bthd', w.astype(v.dtype), v)


def even_output(o_lat, o_b, w_uv, w_out):
    B, T = o_lat.shape[:2]
    o_a = jnp.einsum('bthr,rhv->bthv', o_lat, w_uv)
    return jnp.concatenate([o_a.reshape(B, T, -1), o_b.reshape(B, T, -1)], axis=-1) @ w_out


def banded_attend(q, k, v, nback):
    N, L, H, d = q.shape
    nb = -(-L // Q_BLOCK)
    lp = nb * Q_BLOCK
    nprev = -(-nback // Q_BLOCK)
    qb = jnp.pad(q, ((0, 0), (0, lp - L), (0, 0), (0, 0))).reshape(N, nb, Q_BLOCK, H, d)
    kp = jnp.pad(k, ((0, 0), (nprev * Q_BLOCK, lp - L), (0, 0), (0, 0))).reshape(N, nb + nprev, Q_BLOCK, H, d)
    vp = jnp.pad(v, ((0, 0), (nprev * Q_BLOCK, lp - L), (0, 0), (0, 0))).reshape(N, nb + nprev, Q_BLOCK, H, d)
    kw = jnp.concatenate([kp[:, i:i + nb] for i in range(nprev + 1)], axis=2)
    vw = jnp.concatenate([vp[:, i:i + nb] for i in range(nprev + 1)], axis=2)
    s = jnp.einsum('njqhd,njkhd->njhqk', qb, kw).astype(jnp.float32) * DIL_SCALE
    qi = jnp.arange(Q_BLOCK)[:, None]
    ki = jnp.arange((nprev + 1) * Q_BLOCK)[None, :]
    diff = qi + nprev * Q_BLOCK - ki
    ku = jnp.arange(nb)[:, None, None] * Q_BLOCK - nprev * Q_BLOCK + ki[None]
    mask = (diff >= 0)[None] & (diff <= nback)[None] & (ku >= 0)
    s = jnp.where(mask[None, :, None], s, NEG_INF)
    lse = jax.nn.logsumexp(s, axis=-1)
    p = jnp.exp(s - lse[..., None])
    o = jnp.einsum('njhqk,njkhd->njqhd', p.astype(v.dtype), vw).reshape(N, lp, H, d)[:, :L]
    lse = jnp.swapaxes(lse, 2, 3).reshape(N, lp, H)[:, :L]
    return o, lse


def dilated_prompt(q, k, v, window, dil):
    B, S, H, d = q.shape
    L = S // dil
    strided = lambda a: jnp.swapaxes(a.reshape(B, L, dil, H, d), 1, 2).reshape(B * dil, L, H, d)
    o, lse = banded_attend(strided(q), strided(k), strided(v), window // dil)
    unstrided = lambda a: jnp.swapaxes(a.reshape(B, dil, L, *a.shape[2:]), 1, 2).reshape(B, S, *a.shape[2:])
    return unstrided(o), unstrided(lse)


def dilated_sample(q, k, v, buf, window, dil):
    wc = buf.shape[1]
    T = q.shape[1]
    k_all = jnp.concatenate([buf[:, :, 0], k], axis=1)
    v_all = jnp.concatenate([buf[:, :, 1], v], axis=1)
    rows = wc + jnp.arange(T)[:, None] - dil * jnp.arange(window // dil + 1)[None, :]
    valid = rows >= 0
    rows = jnp.maximum(rows, 0)
    kg = k_all[:, rows]
    vg = v_all[:, rows]
    s = jnp.einsum('bthd,btmhd->bthm', q, kg).astype(jnp.float32) * DIL_SCALE
    s = jnp.where(valid[None, :, None, :], s, NEG_INF)
    lse = jax.nn.logsumexp(s, axis=-1)
    p = jnp.exp(s - lse[..., None])
    o = jnp.einsum('bthm,btmhd->bthd', p.astype(v.dtype), vg)
    new_buf = jnp.concatenate([buf, jnp.stack([k, v], axis=2)], axis=1)[:, -wc:]
    return o, lse, new_buf


def combine_groups(outs, lses):
    a = jax.nn.softmax(jnp.stack(lses, axis=0), axis=0)
    return jnp.einsum('gbth,gbthd->bthd', a.astype(outs[0].dtype), jnp.stack(outs, axis=0))


def mem_kv(mem, g, w_kv):
    B = mem.shape[0]
    kv = (rmsnorm(mem, g) @ w_kv).reshape(B, N_MEM, 2, H_M, DH_M)
    return kv[:, :, 0], kv[:, :, 1]


def mem_attend(h, w_q, k, v, w_o):
    B, T, _ = h.shape
    q = (h @ w_q).reshape(B, T, H_M, DH_M)
    s = jnp.einsum('bthd,bmhd->bhtm', q, k).astype(jnp.float32) * MEM_SCALE
    p = jax.nn.softmax(s, axis=-1)
    o = jnp.einsum('bhtm,bmhd->bthd', p.astype(v.dtype), v)
    return o.reshape(B, T, H_M * DH_M) @ w_o


def setup_inputs(seed: int = 0) -> dict:
    key = jax.random.key(seed)
    ks = iter(jax.random.split(key, 40))

    def nrm(shape, scale=1.0):
        return jax.random.normal(next(ks), shape, jnp.float32) * scale

    def gain(shape):
        return 1.0 + 0.01 * nrm(shape)

    n_pages = PAST_LEN // PAGE_SIZE
    n_pool = (DEC_BATCH * n_pages * 5) // 4
    x_prompt = nrm((BATCH, SEQ, D_MODEL))
    x_sample = nrm((DEC_BATCH, DEC_SEQ, D_MODEL))
    mem_prompt = nrm((BATCH, N_MEM, D_MODEL))
    cache_mla_latent = nrm((N_EVEN, n_pool, PAGE_SIZE, R_KV))
    cache_mla_krope = nrm((N_EVEN, n_pool, PAGE_SIZE, DH_ROPE))
    cache_sb_k = nrm((N_EVEN, n_pool, PAGE_SIZE, H_B, DH_B))
    cache_sb_v = nrm((N_EVEN, n_pool, PAGE_SIZE, H_B, DH_B))
    cache_dil_w128 = nrm((N_ODD, DEC_BATCH, min(DIL_GROUPS[0][0], PAST_LEN), 2, H_C, DH_C))
    cache_dil_w512 = nrm((N_ODD, DEC_BATCH, min(DIL_GROUPS[1][0], PAST_LEN), 2, H_C, DH_C))
    cache_dil_w2048 = nrm((N_ODD, DEC_BATCH, min(DIL_GROUPS[2][0], PAST_LEN), 2, H_C, DH_C))
    cache_mem_k = nrm((DEPTH, DEC_BATCH, N_MEM, H_M, DH_M))
    cache_mem_v = nrm((DEPTH, DEC_BATCH, N_MEM, H_M, DH_M))
    perm = jax.random.permutation(next(ks), n_pool)
    page_table = perm[:DEC_BATCH * n_pages].reshape(DEC_BATCH, n_pages).astype(jnp.int32)
    return {
        'x_prompt': x_prompt, 'x_sample': x_sample, 'mem_prompt': mem_prompt,
        'cache_mla_latent': cache_mla_latent, 'cache_mla_krope': cache_mla_krope,
        'cache_sb_k': cache_sb_k, 'cache_sb_v': cache_sb_v,
        'cache_dil_w128': cache_dil_w128, 'cache_dil_w512': cache_dil_w512, 'cache_dil_w2048': cache_dil_w2048,
        'cache_mem_k': cache_mem_k, 'cache_mem_v': cache_mem_v, 'page_table': page_table,
        'ffn_norm': gain((DEPTH, 2, D_MODEL)),
        'ffn_w_gu': nrm((DEPTH, 2, D_MODEL, 2 * D_FF), D_MODEL ** -0.5),
        'ffn_w_down': nrm((DEPTH, 2, D_FF, D_MODEL), D_FF ** -0.5),
        'mix_norm': gain((DEPTH, D_MODEL)),
        'ev_w_in': nrm((N_EVEN, D_MODEL, EV_IN), D_MODEL ** -0.5),
        'ev_g_q': gain((N_EVEN, R_Q)),
        'ev_g_kv': gain((N_EVEN, R_KV)),
        'ev_w_uq': nrm((N_EVEN, R_Q, H_A, DH_NOPE + DH_ROPE), R_Q ** -0.5),
        'ev_w_uk': nrm((N_EVEN, R_KV, H_A, DH_NOPE), R_KV ** -0.5),
        'ev_w_uv': nrm((N_EVEN, R_KV, H_A, DH_V), R_KV ** -0.5),
        'ev_w_out': nrm((N_EVEN, EV_OUT, D_MODEL), EV_OUT ** -0.5),
        'od_w_in': nrm((N_ODD, D_MODEL, OD_IN), D_MODEL ** -0.5),
        'od_w_out': nrm((N_ODD, OD_OUT, D_MODEL), OD_OUT ** -0.5),
        'xa_norm': gain((DEPTH, D_MODEL)),
        'xa_mem_norm': gain((DEPTH, D_MODEL)),
        'xa_w_q': nrm((DEPTH, D_MODEL, H_M * DH_M), D_MODEL ** -0.5),
        'xa_w_kv': nrm((DEPTH, D_MODEL, 2 * H_M * DH_M), D_MODEL ** -0.5),
        'xa_w_o': nrm((DEPTH, H_M * DH_M, D_MODEL), (H_M * DH_M) ** -0.5),
        'final_norm': gain((D_MODEL,)),
    }


def reference(x_prompt, x_sample, mem_prompt, cache_mla_latent, cache_mla_krope, cache_sb_k, cache_sb_v,
              cache_dil_w128, cache_dil_w512, cache_dil_w2048, cache_mem_k, cache_mem_v, page_table,
              ffn_norm, ffn_w_gu, ffn_w_down, mix_norm,
              ev_w_in, ev_g_q, ev_g_kv, ev_w_uq, ev_w_uk, ev_w_uv, ev_w_out,
              od_w_in, od_w_out,
              xa_norm, xa_mem_norm, xa_w_q, xa_w_kv, xa_w_o, final_norm):
    B, S, _ = x_prompt.shape
    Bd, T, _ = x_sample.shape
    past = page_table.shape[1] * cache_mla_latent.shape[2]
    pos_p = jnp.arange(S, dtype=jnp.int32)
    pos_s = past + jnp.arange(T, dtype=jnp.int32)
    kpos_s = jnp.arange(past + T, dtype=jnp.int32)
    dil_bufs = (cache_dil_w128, cache_dil_w512, cache_dil_w2048)

    p_lat, p_kpe, p_sbk, p_sbv = [], [], [], []
    s_lat, s_kpe, s_sbk, s_sbv = [], [], [], []
    p_dil = [[] for _ in DIL_GROUPS]
    s_dil = [[] for _ in DIL_GROUPS]
    p_memk, p_memv = [], []

    xp, xs = x_prompt, x_sample
    for layer in range(DEPTH):
        xp = xp + swiglu_half(xp, ffn_norm[layer, 0], ffn_w_gu[layer, 0], ffn_w_down[layer, 0])
        xs = xs + swiglu_half(xs, ffn_norm[layer, 0], ffn_w_gu[layer, 0], ffn_w_down[layer, 0])
        hp = rmsnorm(xp, mix_norm[layer])
        hs = rmsnorm(xs, mix_norm[layer])
        i = layer // 2
        if layer % 2 == 0:
            ew = (ev_w_in[i], ev_g_q[i], ev_g_kv[i], ev_w_uq[i], ev_w_uk[i])
            q_lat, q_pe, c_kv, k_pe, q_b, k_b, v_b = even_project(hp, pos_p, *ew)
            o_lat = map_query_blocks(lambda ql, qp, pb: mla_attend(ql, qp, c_kv, k_pe, pb, pos_p), (q_lat, q_pe), pos_p)
            o_b = map_query_blocks(lambda qq, pb: sb_attend(qq, k_b, v_b, pb, pos_p), (q_b,), pos_p)
            xp = xp + even_output(o_lat, o_b, ev_w_uv[i], ev_w_out[i])
            p_lat.append(c_kv)
            p_kpe.append(k_pe)
            p_sbk.append(k_b)
            p_sbv.append(v_b)
            q_lat, q_pe, c_kv, k_pe, q_b, k_b, v_b = even_project(hs, pos_s, *ew)
            c_all = jnp.concatenate([gather_pages(cache_mla_latent, i, page_table), c_kv], axis=1)
            kpe_all = jnp.concatenate([gather_pages(cache_mla_krope, i, page_table), k_pe], axis=1)
            kb_all = jnp.concatenate([gather_pages(cache_sb_k, i, page_table), k_b], axis=1)
            vb_all = jnp.concatenate([gather_pages(cache_sb_v, i, page_table), v_b], axis=1)
            o_lat = mla_attend(q_lat, q_pe, c_all, kpe_all, pos_s, kpos_s)
            o_b = sb_attend(q_b, kb_all, vb_all, pos_s, kpos_s)
            xs = xs + even_output(o_lat, o_b, ev_w_uv[i], ev_w_out[i])
            s_lat.append(c_kv)
            s_kpe.append(k_pe)
            s_sbk.append(k_b)
            s_sbv.append(v_b)
        else:
            zp = (hp @ od_w_in[i]).reshape(B, S, N_GROUPS, 3, H_C, DH_C)
            zs = (hs @ od_w_in[i]).reshape(Bd, T, N_GROUPS, 3, H_C, DH_C)
            outs_p, lses_p, outs_s, lses_s = [], [], [], []
            for g, (window, dil) in enumerate(DIL_GROUPS):
                q = rotary(zp[:, :, g, 0], pos_p)
                k = rotary(zp[:, :, g, 1], pos_p)
                v = zp[:, :, g, 2]
                o, lse = dilated_prompt(q, k, v, window, dil)
                outs_p.append(o)
                lses_p.append(lse)
                p_dil[g].append(jnp.stack([k, v], axis=2)[:, -min(window, S):])
                q = rotary(zs[:, :, g, 0], pos_s)
                k = rotary(zs[:, :, g, 1], pos_s)
                v = zs[:, :, g, 2]
                o, lse, new_buf = dilated_sample(q, k, v, dil_bufs[g][i], window, dil)
                outs_s.append(o)
                lses_s.append(lse)
                s_dil[g].append(new_buf)
            xp = xp + combine_groups(outs_p, lses_p).reshape(B, S, OD_OUT) @ od_w_out[i]
            xs = xs + combine_groups(outs_s, lses_s).reshape(Bd, T, OD_OUT) @ od_w_out[i]
        mk, mv = mem_kv(mem_prompt, xa_mem_norm[layer], xa_w_kv[layer])
        p_memk.append(mk)
        p_memv.append(mv)
        xp = xp + mem_attend(rmsnorm(xp, xa_norm[layer]), xa_w_q[layer], mk, mv, xa_w_o[layer])
        xs = xs + mem_attend(rmsnorm(xs, xa_norm[layer]), xa_w_q[layer], cache_mem_k[layer], cache_mem_v[layer], xa_w_o[layer])
        xp = xp + swiglu_half(xp, ffn_norm[layer, 1], ffn_w_gu[layer, 1], ffn_w_down[layer, 1])
        xs = xs + swiglu_half(xs, ffn_norm[layer, 1], ffn_w_gu[layer, 1], ffn_w_down[layer, 1])

    y_prompt = rmsnorm(xp, final_norm)
    y_sample = rmsnorm(xs, final_norm)
    return (y_prompt, y_sample,
            jnp.stack(p_lat), jnp.stack(p_kpe), jnp.stack(p_sbk), jnp.stack(p_sbv),
            jnp.stack(p_dil[0]), jnp.stack(p_dil[1]), jnp.stack(p_dil[2]),
            jnp.stack(p_memk), jnp.stack(p_memv),
            jnp.stack(s_lat), jnp.stack(s_kpe), jnp.stack(s_sbk), jnp.stack(s_sbv),
            jnp.stack(s_dil[0]), jnp.stack(s_dil[1]), jnp.stack(s_dil[2]))
```

```python
import functools
import math

import jax
import jax.numpy as jnp
from jax import lax
from jax.experimental import pallas as pl
from jax.experimental.pallas import tpu as pltpu

F32 = jnp.float32
BF = jnp.bfloat16

H_A, DH_NOPE, DH_ROPE, DH_V, R_Q, R_KV = 8, 64, 32, 64, 256, 128
H_B, DH_B = 8, 64
DIL_GROUPS = ((128, 1), (512, 4), (2048, 16))
H_C, DH_C = 4, 64
H_M, DH_M = 4, 64
PAGE = 128
ROPE_THETA = 10000.0
RMS_EPS = 1e-6
NEG_INF = -1e30
MLA_SCALE = 1.0 / math.sqrt(DH_NOPE + DH_ROPE)
SB_SCALE = 1.0 / math.sqrt(DH_B)
DIL_SCALE = 1.0 / math.sqrt(DH_C)
MEM_SCALE = 1.0 / math.sqrt(DH_M)

LANES = 128
BLK = 128
SB_DEAD = -104.0
VMEM_LIMIT = 56 * 2**20


def _cp(*sem):
    return pltpu.CompilerParams(dimension_semantics=sem, vmem_limit_bytes=VMEM_LIMIT)


def _resident(shape):
    nd = len(shape)
    return pl.BlockSpec(shape, lambda *_: (0,) * nd, pipeline_mode=pl.Buffered(1))


def _dot(a, b):
    return jnp.dot(a, b, preferred_element_type=F32)


def _dot_nt(a, b):
    return lax.dot_general(a, b, (((1,), (1,)), ((), ())), preferred_element_type=F32)


def _rms(x, g):
    return x * lax.rsqrt(jnp.mean(x * x, axis=-1, keepdims=True) + RMS_EPS) * g


def _rope(x, cos, sin_signed, half):
    n = x.shape[-1]
    lane = lax.broadcasted_iota(jnp.int32, x.shape, 1)
    first = (lane % (2 * half)) < half
    rot = jnp.where(first, pltpu.roll(x, n - half, 1), pltpu.roll(x, half, 1))
    return x * cos + rot * sin_signed


def _log_sigmoid_pair(z):
    sp = jnp.log1p(jnp.exp(-jnp.abs(z)))
    return jnp.minimum(z, 0.0) - sp, jnp.minimum(-z, 0.0) - sp


def _head_mask(shape, h, dh):
    lane = lax.broadcasted_iota(jnp.int32, shape, len(shape) - 1)
    return (lane >= h * dh) & (lane < (h + 1) * dh)


def _suffix_sum(lk, upper):
    hi = lk.astype(BF)
    lo = (lk - hi.astype(F32)).astype(BF)
    return _dot(hi, upper) + _dot(lo, upper)


def _strict_upper(n):
    r = lax.broadcasted_iota(jnp.int32, (n, n), 0)
    c = lax.broadcasted_iota(jnp.int32, (n, n), 1)
    return jnp.where(r > c, 1.0, 0.0).astype(BF)


def _ffn_body(x_ref, g_ref, wgu_ref, wd_ref, gf_ref, o_ref, a_scr, *, d_ff, chunk, final):
    x = x_ref[...]
    h = _rms(x, g_ref[...]).astype(BF)
    for c in range(d_ff // chunk):
        gate = _dot(h, wgu_ref[:, c * chunk:(c + 1) * chunk])
        up = _dot(h, wgu_ref[:, d_ff + c * chunk:d_ff + (c + 1) * chunk])
        a_scr[:, c * chunk:(c + 1) * chunk] = (gate / (1.0 + jnp.exp(-gate)) * up).astype(BF)
    y = x + 0.5 * _dot(a_scr[...], wd_ref[...])
    if final:
        y = _rms(y, gf_ref[...])
    o_ref[...] = y


def _ffn(x, g, wgu, wd, gf, *, final):
    t, d = x.shape
    f = wd.shape[0]
    tm = min(512, t)
    return pl.pallas_call(
        functools.partial(_ffn_body, d_ff=f, chunk=256, final=final),
        grid=(t // tm,),
        in_specs=[pl.BlockSpec((tm, d), lambda i: (i, 0)), _resident((1, d)), _resident((d, 2 * f)),
                  _resident((f, d)), _resident((1, d))],
        out_specs=pl.BlockSpec((tm, d), lambda i: (i, 0)),
        out_shape=jax.ShapeDtypeStruct((t, d), F32),
        scratch_shapes=[pltpu.VMEM((tm, f), BF)],
        compiler_params=_cp("parallel"), name="ffn_half")(x, g, wgu, wd, gf)


def _norm_mm_body(x_ref, g_ref, w_ref, k_ref, v_ref):
    y = _dot(_rms(x_ref[...], g_ref[...]).astype(BF), w_ref[...])
    n = k_ref.shape[-1]
    k_ref[...] = y[:, :n]
    v_ref[...] = y[:, n:]


def _mem_kv(mem, g, w):
    t, d = mem.shape
    n = w.shape[1] // 2
    tm = min(512, t)
    return pl.pallas_call(
        _norm_mm_body, grid=(t // tm,),
        in_specs=[pl.BlockSpec((tm, d), lambda i: (i, 0)), _resident((1, d)), _resident((d, 2 * n))],
        out_specs=[pl.BlockSpec((tm, n), lambda i: (i, 0))] * 2,
        out_shape=[jax.ShapeDtypeStruct((t, n), F32)] * 2,
        compiler_params=_cp("parallel"), name="mem_kv")(mem, g, w)


def _even_proj_body(x_ref, g_ref, win_ref, gq_ref, gkv_ref, wuq_ref, wuk_ref, cos_ref, sin_ref,
                    qm_ref, km_ref, lat_ref, kpe_ref, qb_ref, kbb_ref, vbb_ref, kb_ref, vb_ref):
    h = _rms(x_ref[...], g_ref[...]).astype(BF)
    z = _dot(h, win_ref[...])
    cos, sin = cos_ref[...], sin_ref[...]
    c_q = _rms(z[:, :R_Q], gq_ref[...]).astype(BF)
    c_kv = _rms(z[:, R_Q:R_Q + R_KV], gkv_ref[...])
    kpe = _rope(z[:, 384:512], cos, sin, DH_ROPE // 2)
    lat_ref[...] = c_kv
    kpe_ref[...] = kpe[:, :DH_ROPE]
    km_ref[:, :R_KV] = c_kv.astype(BF)
    km_ref[:, R_KV:] = kpe.astype(BF)
    qb_ref[...] = z[:, 512:1024].astype(BF)
    kb = z[:, 1024:1536]
    vb = z[:, 1536:2048]
    kb_ref[...] = kb
    vb_ref[...] = vb
    kbb_ref[...] = kb.astype(BF)
    vbb_ref[...] = vb.astype(BF)
    q = _dot(c_q, wuq_ref[...])
    for hp in range(H_A // 2):
        ql = _dot(q[:, hp * 128:(hp + 1) * 128].astype(BF), wuk_ref[hp])
        for e in range(2):
            hh = 2 * hp + e
            qr = _rope(q[:, 512 + hh * 128:512 + (hh + 1) * 128], cos, sin, DH_ROPE // 2)
            qm_ref[hh, :, :R_KV] = ql[:, e * 128:(e + 1) * 128].astype(BF)
            qm_ref[hh, :, R_KV:] = qr.astype(BF)


def _even_project(x, g, win, gq, gkv, wuq, wuk, cos, sin):
    t, d = x.shape
    tm = min(512, t)
    nper = cos.shape[0] // tm
    row = lambda i: (i, 0)
    outs = [((H_A, t, 256), BF, pl.BlockSpec((H_A, tm, 256), lambda i: (0, i, 0))),
            ((t, 256), BF, pl.BlockSpec((tm, 256), row)),
            ((t, R_KV), F32, pl.BlockSpec((tm, R_KV), row)),
            ((t, DH_ROPE), F32, pl.BlockSpec((tm, DH_ROPE), row)),
            ((t, 512), BF, pl.BlockSpec((tm, 512), row)),
            ((t, 512), BF, pl.BlockSpec((tm, 512), row)),
            ((t, 512), BF, pl.BlockSpec((tm, 512), row)),
            ((t, 512), F32, pl.BlockSpec((tm, 512), row)),
            ((t, 512), F32, pl.BlockSpec((tm, 512), row))]
    return pl.pallas_call(
        _even_proj_body, grid=(t // tm,),
        in_specs=[pl.BlockSpec((tm, d), row), _resident((1, d)), _resident(win.shape), _resident((1, R_Q)),
                  _resident((1, R_KV)), _resident(wuq.shape), _resident(wuk.shape),
                  pl.BlockSpec((tm, LANES), lambda i: (i % nper, 0)),
                  pl.BlockSpec((tm, LANES), lambda i: (i % nper, 0))],
        out_specs=[o[2] for o in outs],
        out_shape=[jax.ShapeDtypeStruct(o[0], o[1]) for o in outs],
        compiler_params=_cp("parallel"), name="even_project")(x, g, win, gq, gkv, wuq, wuk, cos, sin)


def _mla_prompt_body(q_ref, kv_ref, o_ref, m_scr, l_scr, acc_scr):
    i = pl.program_id(1)
    rows = H_A * BLK
    q = q_ref[...].reshape(rows, 256)
    m_scr[...] = jnp.full((rows, LANES), NEG_INF, F32)
    l_scr[...] = jnp.zeros((rows, LANES), F32)
    acc_scr[...] = jnp.zeros((rows, R_KV), F32)
    qpos = i * BLK + lax.broadcasted_iota(jnp.int32, (rows, BLK), 0) % BLK
    col = lax.broadcasted_iota(jnp.int32, (rows, BLK), 1)

    def body(j, carry):
        k = kv_ref[pl.ds(pl.multiple_of(j * BLK, BLK), BLK), :]
        s = _dot_nt(q, k) * MLA_SCALE
        s = jnp.where(j * BLK + col <= qpos, s, NEG_INF)
        m_prev = m_scr[...]
        m_new = jnp.maximum(m_prev, jnp.max(s, axis=1, keepdims=True))
        p = jnp.exp(s - m_new)
        alpha = jnp.exp(m_prev - m_new)
        l_scr[...] = alpha * l_scr[...] + jnp.sum(p, axis=1, keepdims=True)
        acc_scr[...] = alpha * acc_scr[...] + _dot(p.astype(BF), k[:, :R_KV])
        m_scr[...] = m_new
        return carry

    lax.fori_loop(0, i + 1, body, 0)
    o_ref[...] = (acc_scr[...] / l_scr[...]).reshape(H_A, BLK, R_KV).astype(BF)


def _mla_prompt(qm, km, b, s):
    nq = s // BLK
    rows = H_A * BLK
    return pl.pallas_call(
        _mla_prompt_body, grid=(b, nq),
        in_specs=[pl.BlockSpec((H_A, BLK, 256), lambda bb, i: (0, bb * nq + i, 0)),
                  pl.BlockSpec((s, 256), lambda bb, i: (bb, 0))],
        out_specs=pl.BlockSpec((H_A, BLK, R_KV), lambda bb, i: (0, bb * nq + i, 0)),
        out_shape=jax.ShapeDtypeStruct((H_A, b * s, R_KV), BF),
        scratch_shapes=[pltpu.VMEM((rows, LANES), F32), pltpu.VMEM((rows, LANES), F32),
                        pltpu.VMEM((rows, R_KV), F32)],
        compiler_params=_cp("parallel", "parallel"), name="mla_prompt")(qm, km)


def _sb_block(qe, k, v, valid, c, upper):
    z = _dot_nt(qe, k) * SB_SCALE
    ls_pos, ls_neg = _log_sigmoid_pair(z)
    if valid is not None:
        ls_neg = jnp.where(valid, ls_neg, 0.0)
    w = jnp.exp(ls_pos + _suffix_sum(ls_neg, upper) + c)
    if valid is not None:
        w = jnp.where(valid, w, 0.0)
    return _dot(w.astype(BF), v), c + jnp.sum(ls_neg, axis=1, keepdims=True)


def _sb_prompt_body(q_ref, k_ref, v_ref, o_ref):
    i = pl.program_id(2)
    q = q_ref[...].astype(F32)
    first = _head_mask((BLK, LANES), 0, DH_B)
    q0 = jnp.where(first, q, 0.0).astype(BF)
    q1 = jnp.where(first, 0.0, q).astype(BF)
    upper = _strict_upper(BLK)
    row = lax.broadcasted_iota(jnp.int32, (BLK, BLK), 0)
    col = lax.broadcasted_iota(jnp.int32, (BLK, BLK), 1)

    def body(jj, carry):
        c0, c1, a0, a1 = carry
        j = i - jj
        off = pl.multiple_of(j * BLK, BLK)
        k = k_ref[pl.ds(off, BLK), :]
        v = v_ref[pl.ds(off, BLK), :]
        valid = (j * BLK + col) < (i * BLK + row)
        d0, c0 = _sb_block(q0, k, v, valid, c0, upper)
        d1, c1 = _sb_block(q1, k, v, valid, c1, upper)
        return c0, c1, a0 + d0, a1 + d1

    zc = jnp.zeros((BLK, 1), F32)
    za = jnp.zeros((BLK, LANES), F32)
    _, _, a0, a1 = lax.fori_loop(0, i + 1, body, (zc, zc, za, za))
    o_ref[...] = jnp.where(first, a0, a1).astype(BF)


def _sb_prompt(qb, kb, vb, b, s):
    nq = s // BLK
    return pl.pallas_call(
        _sb_prompt_body, grid=(b, H_B // 2, nq),
        in_specs=[pl.BlockSpec((BLK, LANES), lambda bb, hp, i: (bb * nq + i, hp)),
                  pl.BlockSpec((s, LANES), lambda bb, hp, i: (bb, hp)),
                  pl.BlockSpec((s, LANES), lambda bb, hp, i: (bb, hp))],
        out_specs=pl.BlockSpec((BLK, LANES), lambda bb, hp, i: (bb * nq + i, hp)),
        out_shape=jax.ShapeDtypeStruct((b * s, H_B * DH_B), BF),
        compiler_params=_cp("parallel", "parallel", "parallel"), name="sb_prompt")(qb, kb, vb)


def _even_out_body(x_ref, ol_ref, ob_ref, wuv_ref, wout_ref, o_ref):
    parts = []
    for hp in range(H_A // 2):
        ol = jnp.concatenate([ol_ref[2 * hp], ol_ref[2 * hp + 1]], axis=1)
        parts.append(_dot(ol, wuv_ref[hp]).astype(BF))
    oa = jnp.concatenate(parts, axis=1)
    na = H_A * DH_V
    o_ref[...] = x_ref[...] + _dot(oa, wout_ref[:na, :]) + _dot(ob_ref[...], wout_ref[na:, :])


def _even_output(x, olat, ob, wuv, wout):
    t, d = x.shape
    tm = min(512, t)
    row = lambda i: (i, 0)
    return pl.pallas_call(
        _even_out_body, grid=(t // tm,),
        in_specs=[pl.BlockSpec((tm, d), row), pl.BlockSpec((H_A, tm, R_KV), lambda i: (0, i, 0)),
                  pl.BlockSpec((tm, 512), row), _resident(wuv.shape), _resident(wout.shape)],
        out_specs=pl.BlockSpec((tm, d), row),
        out_shape=jax.ShapeDtypeStruct((t, d), F32),
        compiler_params=_cp("parallel"), name="even_output")(x, olat, ob, wuv, wout)


def _online_softmax_step(s, v, m_scr, l_scr, acc_scr):
    m_prev = m_scr[...]
    m_new = jnp.maximum(m_prev, jnp.max(s, axis=1, keepdims=True))
    p = jnp.exp(s - m_new[:, :1])
    alpha = jnp.exp(m_prev - m_new)
    l_scr[...] = alpha * l_scr[...] + jnp.sum(p, axis=1, keepdims=True)
    acc_scr[...] = alpha * acc_scr[...] + _dot(p.astype(BF), v)
    m_scr[...] = m_new


def _mla_sample_body(pt_ref, q_ref, new_ref, *rest, group, t_new):
    lat_refs = rest[:group]
    kr_refs = rest[group:2 * group]
    o_ref, m_scr, l_scr, acc_scr = rest[2 * group:]
    p = pl.program_id(1)
    rows = q_ref.shape[1]

    @pl.when(p == 0)
    def _():
        m_scr[...] = jnp.full((rows, LANES), NEG_INF, F32)
        l_scr[...] = jnp.zeros((rows, LANES), F32)
        acc_scr[...] = jnp.zeros((rows, R_KV), F32)

    q = q_ref[0]
    ql = q[:, :R_KV]
    qr = q[:, R_KV:R_KV + DH_ROPE]
    for g in range(group):
        lat = lat_refs[g][0].astype(BF)
        kr = kr_refs[g][0].astype(BF)
        s = (_dot_nt(ql, lat) + _dot_nt(qr, kr)) * MLA_SCALE
        _online_softmax_step(s, lat, m_scr, l_scr, acc_scr)

    @pl.when(p == pl.num_programs(1) - 1)
    def _():
        qf = q.astype(F32)
        tq = lax.broadcasted_iota(jnp.int32, (rows, 1), 0) // H_A
        m = m_scr[...]
        l = l_scr[...]
        acc = acc_scr[...]
        for j in range(t_new):
            kn = new_ref[0, j:j + 1, :].astype(F32)
            s = jnp.sum(qf * kn, axis=1, keepdims=True) * MLA_SCALE
            s = jnp.where(j <= tq, s, NEG_INF)
            m_new = jnp.maximum(m, s)
            pj = jnp.exp(s - m_new)
            alpha = jnp.exp(m - m_new)
            l = alpha * l + pj
            acc = alpha * acc + pj.astype(BF).astype(F32) * kn[:, :R_KV]
            m = m_new
        o_ref[0] = (acc / l).astype(BF)


def _mla_sample(page_table, q, new, lat_pool, kr_pool, group):
    bd, n_pages = page_table.shape
    rows = q.shape[1]
    t_new = new.shape[1]
    lat_specs = [pl.BlockSpec((1, PAGE, R_KV), functools.partial(
        lambda b, p, pt, g: (pt[b, p * group + g], 0, 0), g=g)) for g in range(group)]
    kr_specs = [pl.BlockSpec((1, PAGE, DH_ROPE), functools.partial(
        lambda b, p, pt, g: (pt[b, p * group + g], 0, 0), g=g)) for g in range(group)]
    grid_spec = pltpu.PrefetchScalarGridSpec(
        num_scalar_prefetch=1, grid=(bd, n_pages // group),
        in_specs=[pl.BlockSpec((1, rows, 256), lambda b, p, pt: (b, 0, 0)),
                  pl.BlockSpec((1, t_new, 256), lambda b, p, pt: (b, 0, 0))] + lat_specs + kr_specs,
        out_specs=pl.BlockSpec((1, rows, R_KV), lambda b, p, pt: (b, 0, 0)),
        scratch_shapes=[pltpu.VMEM((rows, LANES), F32), pltpu.VMEM((rows, LANES), F32),
                        pltpu.VMEM((rows, R_KV), F32)])
    return pl.pallas_call(
        functools.partial(_mla_sample_body, group=group, t_new=t_new), grid_spec=grid_spec,
        out_shape=jax.ShapeDtypeStruct((bd, rows, R_KV), BF),
        compiler_params=_cp("parallel", "arbitrary"), name="mla_sample")(
            page_table, q, new, *([lat_pool] * group), *([kr_pool] * group))


def _sb_sample_body(pt_ref, q_ref, kn_ref, vn_ref, *rest, group, t_new):
    k_refs = rest[:group]
    v_refs = rest[group:2 * group]
    o_ref, c_scr, acc_scr = rest[2 * group:]
    p = pl.program_id(1)
    rows = q_ref.shape[1]
    width = H_B * DH_B
    q = q_ref[0]

    @pl.when(p == 0)
    def _():
        qf = q.astype(F32)
        tq = lax.broadcasted_iota(jnp.int32, (rows, 1), 0) // H_B
        c = jnp.zeros((rows, 1), F32)
        acc = jnp.zeros((rows, width), F32)
        for j in reversed(range(t_new)):
            kn = kn_ref[0, j:j + 1, :].astype(BF).astype(F32)
            vn = vn_ref[0, j:j + 1, :].astype(BF).astype(F32)
            z = jnp.sum(qf * kn, axis=1, keepdims=True) * SB_SCALE
            ls_pos, ls_neg = _log_sigmoid_pair(z)
            valid = j < tq
            w = jnp.where(valid, jnp.exp(ls_pos + c), 0.0)
            acc = acc + w.astype(BF).astype(F32) * vn
            c = c + jnp.where(valid, ls_neg, 0.0)
        c_scr[...] = jnp.broadcast_to(c, (rows, LANES))
        acc_scr[...] = acc

    upper = _strict_upper(PAGE)
    for g in range(group):
        c = c_scr[...]

        @pl.when(jnp.max(c) > SB_DEAD)
        def _():
            d, c_new = _sb_block(q, k_refs[g][0].astype(BF), v_refs[g][0].astype(BF), None, c[:, :1], upper)
            acc_scr[...] = acc_scr[...] + d
            c_scr[...] = jnp.broadcast_to(c_new, (rows, LANES))

    @pl.when(p == pl.num_programs(1) - 1)
    def _():
        hrow = lax.broadcasted_iota(jnp.int32, (H_B, width), 0)
        hlane = lax.broadcasted_iota(jnp.int32, (H_B, width), 1) // DH_B
        for t in range(t_new):
            a = acc_scr[t * H_B:(t + 1) * H_B, :]
            o_ref[0, t:t + 1, :] = jnp.sum(jnp.where(hrow == hlane, a, 0.0), axis=0, keepdims=True)


def _sb_sample(page_table, qbd, kn, vn, k_pool, v_pool, group):
    bd, n_pages = page_table.shape
    rows = qbd.shape[1]
    t_new = kn.shape[1]
    width = H_B * DH_B

    def page(b, p, pt, g):
        return (pt[b, n_pages - 1 - (p * group + g)], 0, 0)

    k_specs = [pl.BlockSpec((1, PAGE, width), functools.partial(page, g=g)) for g in range(group)]
    v_specs = [pl.BlockSpec((1, PAGE, width), functools.partial(page, g=g)) for g in range(group)]
    own = lambda b, p, pt: (b, 0, 0)
    grid_spec = pltpu.PrefetchScalarGridSpec(
        num_scalar_prefetch=1, grid=(bd, n_pages // group),
        in_specs=[pl.BlockSpec((1, rows, width), own), pl.BlockSpec((1, t_new, width), own),
                  pl.BlockSpec((1, t_new, width), own)] + k_specs + v_specs,
        out_specs=pl.BlockSpec((1, t_new, width), own),
        scratch_shapes=[pltpu.VMEM((rows, LANES), F32), pltpu.VMEM((rows, width), F32)])
    return pl.pallas_call(
        functools.partial(_sb_sample_body, group=group, t_new=t_new), grid_spec=grid_spec,
        out_shape=jax.ShapeDtypeStruct((bd, t_new, width), F32),
        compiler_params=_cp("parallel", "arbitrary"), name="sb_sample")(
            page_table, qbd, kn, vn, *([k_pool] * group), *([v_pool] * group))


def _odd_proj_body(x_ref, g_ref, win_ref, cos_ref, sin_ref, z_ref):
    h = _rms(x_ref[...], g_ref[...]).astype(BF)
    z = _dot(h, win_ref[...])
    cos, sin = cos_ref[...], sin_ref[...]
    w = H_C * DH_C
    for c in range(z.shape[1] // w):
        part = z[:, c * w:(c + 1) * w]
        if c % 3 != 2:
            part = _rope(part, cos, sin, DH_C // 2)
        z_ref[:, c * w:(c + 1) * w] = part


def _odd_project(x, g, win, cos, sin):
    t, d = x.shape
    n = win.shape[1]
    tm = min(512, t)
    nper = cos.shape[0] // tm
    w = H_C * DH_C
    return pl.pallas_call(
        _odd_proj_body, grid=(t // tm,),
        in_specs=[pl.BlockSpec((tm, d), lambda i: (i, 0)), _resident((1, d)), _resident(win.shape),
                  pl.BlockSpec((tm, w), lambda i: (i % nper, 0)), pl.BlockSpec((tm, w), lambda i: (i % nper, 0))],
        out_specs=pl.BlockSpec((tm, n), lambda i: (i, 0)),
        out_shape=jax.ShapeDtypeStruct((t, n), F32),
        compiler_params=_cp("parallel"), name="odd_project")(x, g, win, cos, sin)


def _dil_prompt_body(q_ref, kc_ref, kp_ref, vc_ref, vp_ref, o_ref, lse_ref, *, nback):
    j = pl.program_id(2)
    w = H_C * DH_C
    q = q_ref[0]
    kk = jnp.concatenate([kp_ref[0], kc_ref[0]], axis=0).astype(BF)
    vv = jnp.concatenate([vp_ref[0], vc_ref[0]], axis=0).astype(BF)
    row = lax.broadcasted_iota(jnp.int32, (BLK, 2 * BLK), 0)
    col = lax.broadcasted_iota(jnp.int32, (BLK, 2 * BLK), 1)
    diff = row + BLK - col
    first_col = jnp.where(j > 0, 0, BLK)
    mask = (diff >= 0) & (diff <= nback) & (col >= first_col)
    o = jnp.zeros((BLK, w), F32)
    lse = jnp.zeros((BLK, w), F32)
    for h in range(H_C):
        hm = _head_mask((BLK, w), h, DH_C)
        s = _dot_nt(jnp.where(hm, q, 0.0).astype(BF), kk) * DIL_SCALE
        s = jnp.where(mask, s, NEG_INF)
        m = jnp.max(s, axis=1, keepdims=True)
        p = jnp.exp(s - m)
        l = jnp.sum(p, axis=1, keepdims=True)
        o = jnp.where(hm, _dot(p.astype(BF), vv) / l, o)
        lse = jnp.where(hm, m + jnp.log(l), lse)
    o_ref[0] = o
    lse_ref[0] = lse


def _dil_prompt(zr, b, s, g, dil, nback):
    w = H_C * DH_C
    ncol = zr.shape[1] // w
    l = s // dil
    nb = l // BLK
    z3 = zr.reshape(b, l, dil * ncol * w)
    base = 3 * g

    def spec(part, prev):
        if prev:
            return pl.BlockSpec((1, BLK, w), lambda bb, r, j: (bb, jnp.maximum(j - 1, 0), r * ncol + base + part))
        return pl.BlockSpec((1, BLK, w), lambda bb, r, j: (bb, j, r * ncol + base + part))

    out_spec = pl.BlockSpec((1, BLK, w), lambda bb, r, j: (bb, j, r))
    o, lse = pl.pallas_call(
        functools.partial(_dil_prompt_body, nback=nback), grid=(b, dil, nb),
        in_specs=[spec(0, False), spec(1, False), spec(1, True), spec(2, False), spec(2, True)],
        out_specs=[out_spec, out_spec],
        out_shape=[jax.ShapeDtypeStruct((b, l, dil * w), F32)] * 2,
        compiler_params=_cp("parallel", "parallel", "parallel"), name=f"dil_prompt_{g}")(z3, z3, z3, z3, z3)
    return o.reshape(b * s, w), lse.reshape(b * s, w)


def _dil_sample_body(qbd_ref, z_ref, b0_ref, b1_ref, b2_ref, o_ref, lse_ref, *, t_new):
    w = H_C * DH_C
    bufs = (b0_ref, b1_ref, b2_ref)
    hrow = lax.broadcasted_iota(jnp.int32, (8, w), 0)
    hlane = lax.broadcasted_iota(jnp.int32, (8, w), 1) // DH_C
    pick = hrow == hlane
    col = lax.broadcasted_iota(jnp.int32, (8, BLK), 1)
    for g, (_, dil) in enumerate(DIL_GROUPS):
        knew = z_ref[0, :, (3 * g + 1) * w:(3 * g + 2) * w].astype(BF).astype(F32)
        vnew = z_ref[0, :, (3 * g + 2) * w:(3 * g + 3) * w].astype(BF).astype(F32)
        for t in range(t_new):
            qbd = qbd_ref[0, (g * t_new + t) * 8:(g * t_new + t + 1) * 8, :].astype(BF)
            qf = qbd.astype(F32)
            off = 0 if dil == 1 else t * 2 * w
            kb = bufs[g][0, :, off:off + w].astype(BF)
            vb = bufs[g][0, :, off + w:off + 2 * w].astype(BF)
            s_buf = _dot_nt(qbd, kb) * DIL_SCALE
            if dil == 1:
                s_buf = jnp.where(col >= t, s_buf, NEG_INF)
            js = list(range(t + 1)) if dil == 1 else [t]
            s_new = [jnp.sum(qf * knew[j:j + 1, :], axis=1, keepdims=True) * DIL_SCALE for j in js]
            m = jnp.max(s_buf, axis=1, keepdims=True)
            for sn in s_new:
                m = jnp.maximum(m, sn)
            p_buf = jnp.exp(s_buf - m)
            l = jnp.sum(p_buf, axis=1, keepdims=True)
            o = _dot(p_buf.astype(BF), vb)
            for j, sn in zip(js, s_new):
                pj = jnp.exp(sn - m)
                l = l + pj
                o = o + pj.astype(BF).astype(F32) * vnew[j:j + 1, :]
            o = o / l
            lse = jnp.broadcast_to(m + jnp.log(l), (8, w))
            o_ref[0, t:t + 1, g * w:(g + 1) * w] = jnp.sum(jnp.where(pick, o, 0.0), axis=0, keepdims=True)
            lse_ref[0, t:t + 1, g * w:(g + 1) * w] = jnp.sum(jnp.where(pick, lse, 0.0), axis=0, keepdims=True)


def _dil_sample(qbd, zr, bufs):
    bd, t_new, n = zr.shape
    w = H_C * DH_C
    own = lambda b: (b, 0, 0)
    buf_specs = []
    for buf, (_, dil) in zip(bufs, DIL_GROUPS):
        cols = 2 * w if dil == 1 else t_new * 2 * w
        buf_specs.append(pl.BlockSpec((1, BLK, cols), own))
    ng = len(DIL_GROUPS)
    return pl.pallas_call(
        functools.partial(_dil_sample_body, t_new=t_new), grid=(bd,),
        in_specs=[pl.BlockSpec((1, qbd.shape[1], w), own), pl.BlockSpec((1, t_new, n), own)] + buf_specs,
        out_specs=[pl.BlockSpec((1, t_new, ng * w), own)] * 2,
        out_shape=[jax.ShapeDtypeStruct((bd, t_new, ng * w), F32)] * 2,
        compiler_params=_cp("parallel"), name="dil_sample")(qbd, zr, *bufs)


def _odd_out_body(x_ref, *rest):
    ng = len(DIL_GROUPS)
    o_refs, l_refs = rest[:ng], rest[ng:2 * ng]
    w_ref, out_ref = rest[2 * ng:]
    ls = [r[...] for r in l_refs]
    m = functools.reduce(jnp.maximum, ls)
    es = [jnp.exp(l - m) for l in ls]
    den = functools.reduce(lambda a, b: a + b, es)
    mix = functools.reduce(lambda a, b: a + b, [e / den * r[...] for e, r in zip(es, o_refs)])
    out_ref[...] = x_ref[...] + _dot(mix.astype(BF), w_ref[...])


def _odd_output(x, outs, lses, wout):
    t, d = x.shape
    w = H_C * DH_C
    tm = min(512, t)
    row = lambda i: (i, 0)
    return pl.pallas_call(
        _odd_out_body, grid=(t // tm,),
        in_specs=[pl.BlockSpec((tm, d), row)] + [pl.BlockSpec((tm, w), row)] * (2 * len(outs)) + [_resident(wout.shape)],
        out_specs=pl.BlockSpec((tm, d), row),
        out_shape=jax.ShapeDtypeStruct((t, d), F32),
        compiler_params=_cp("parallel"), name="odd_output")(x, *outs, *lses, wout)


def _mem_attn_body(x_ref, g_ref, wq_ref, k_ref, v_ref, wo_ref, o_ref):
    x = x_ref[0]
    w = H_M * DH_M
    q = _dot(_rms(x, g_ref[...]).astype(BF), wq_ref[...])
    kk = k_ref[0].astype(BF)
    vv = v_ref[0].astype(BF)
    o = jnp.zeros((x.shape[0], w), F32)
    for h in range(H_M):
        hm = _head_mask(q.shape, h, DH_M)
        s = _dot_nt(jnp.where(hm, q, 0.0).astype(BF), kk) * MEM_SCALE
        m = jnp.max(s, axis=1, keepdims=True)
        p = jnp.exp(s - m)
        l = jnp.sum(p, axis=1, keepdims=True)
        o = jnp.where(hm, _dot(p.astype(BF), vv) / l, o)
    o_ref[0] = x + _dot(o.astype(BF), wo_ref[...])


def _mem_attend(x, g, wq, k, v, wo):
    b, s, d = x.shape
    n_mem, w = k.shape[1:]
    tm = min(512, s)
    return pl.pallas_call(
        _mem_attn_body, grid=(b, s // tm),
        in_specs=[pl.BlockSpec((1, tm, d), lambda bb, i: (bb, i, 0)), _resident((1, d)), _resident(wq.shape),
                  pl.BlockSpec((1, n_mem, w), lambda bb, i: (bb, 0, 0)),
                  pl.BlockSpec((1, n_mem, w), lambda bb, i: (bb, 0, 0)), _resident(wo.shape)],
        out_specs=pl.BlockSpec((1, tm, d), lambda bb, i: (bb, i, 0)),
        out_shape=jax.ShapeDtypeStruct((b, s, d), F32),
        compiler_params=_cp("parallel", "parallel"), name="mem_attend")(x, g, wq, k, v, wo)


def _rope_tables(pos, half, width):
    inv_freq = ROPE_THETA ** (-jnp.arange(half, dtype=F32) / half)
    ang = pos.astype(F32)[:, None] * inv_freq[None, :]
    cos, sin = jnp.cos(ang), jnp.sin(ang)
    reps = width // (2 * half)
    return (jnp.tile(jnp.concatenate([cos, cos], axis=1), (1, reps)),
            jnp.tile(jnp.concatenate([-sin, sin], axis=1), (1, reps)))


def _even_weights(w_in, w_uq, w_uk, w_uv):
    d = w_in.shape[0]
    c0 = R_Q + R_KV
    win = jnp.concatenate([w_in[:, :c0], w_in[:, c0:c0 + DH_ROPE], jnp.zeros((d, LANES - DH_ROPE), F32),
                           w_in[:, c0 + DH_ROPE:]], axis=1).astype(BF)
    nope = w_uq[:, :, :DH_NOPE].reshape(R_Q, H_A * DH_NOPE)
    rope = jnp.pad(w_uq[:, :, DH_NOPE:], ((0, 0), (0, 0), (0, LANES - DH_ROPE))).reshape(R_Q, H_A * LANES)
    wuq = jnp.concatenate([nope, rope], axis=1).astype(BF)
    uk = jnp.transpose(w_uk, (1, 2, 0)).reshape(H_A // 2, 2, DH_NOPE, R_KV)
    z = jnp.zeros_like(uk[:, 0])
    wuk = jnp.concatenate([jnp.concatenate([uk[:, 0], z], axis=2),
                           jnp.concatenate([z, uk[:, 1]], axis=2)], axis=1).astype(BF)
    uv = jnp.transpose(w_uv, (1, 0, 2)).reshape(H_A // 2, 2, R_KV, DH_V)
    zv = jnp.zeros_like(uv[:, 0])
    wuv = jnp.concatenate([jnp.concatenate([uv[:, 0], zv], axis=2),
                           jnp.concatenate([zv, uv[:, 1]], axis=2)], axis=1).astype(BF)
    return win, wuq, wuk, wuv


def _block_diag_rows(q, n_heads, dh):
    lane_head = jnp.arange(n_heads * dh) // dh
    keep = lane_head[None, :] == jnp.arange(n_heads)[:, None]
    return jnp.where(keep, q[..., None, :], jnp.zeros((), q.dtype))


def kernel(x_prompt, x_sample, mem_prompt, cache_mla_latent, cache_mla_krope, cache_sb_k, cache_sb_v,
           cache_dil_w128, cache_dil_w512, cache_dil_w2048, cache_mem_k, cache_mem_v, page_table,
           ffn_norm, ffn_w_gu, ffn_w_down, mix_norm,
           ev_w_in, ev_g_q, ev_g_kv, ev_w_uq, ev_w_uk, ev_w_uv, ev_w_out,
           od_w_in, od_w_out,
           xa_norm, xa_mem_norm, xa_w_q, xa_w_kv, xa_w_o, final_norm):
    b, s, d = x_prompt.shape
    bd, t_new, _ = x_sample.shape
    n_mem = mem_prompt.shape[1]
    depth = mix_norm.shape[0]
    n_pages = page_table.shape[1]
    past = n_pages * PAGE
    pos_p = jnp.arange(s, dtype=jnp.int32)
    pos_s = jnp.tile(past + jnp.arange(t_new, dtype=jnp.int32), bd)
    dil_bufs = (cache_dil_w128, cache_dil_w512, cache_dil_w2048)
    w_hc = H_C * DH_C

    xp = x_prompt.reshape(b * s, d)
    xs = x_sample.reshape(bd * t_new, d)
    mem = mem_prompt.reshape(b * n_mem, d)
    row = lambda v: v.reshape(1, -1)

    p_lat, p_kpe, p_sbk, p_sbv, s_lat, s_kpe, s_sbk, s_sbv = ([] for _ in range(8))
    p_dil = [[] for _ in DIL_GROUPS]
    s_dil = [[] for _ in DIL_GROUPS]
    p_memk, p_memv = [], []

    for layer in range(depth):
        i = layer // 2
        wgu, wd = ffn_w_gu[layer, 0].astype(BF), ffn_w_down[layer, 0].astype(BF)
        xp = _ffn(xp, row(ffn_norm[layer, 0]), wgu, wd, row(final_norm), final=False)
        xs = _ffn(xs, row(ffn_norm[layer, 0]), wgu, wd, row(final_norm), final=False)
        if layer % 2 == 0:
            win, wuq, wuk, wuv = _even_weights(ev_w_in[i], ev_w_uq[i], ev_w_uk[i], ev_w_uv[i])
            wout = ev_w_out[i].astype(BF)
            args = (row(mix_norm[layer]), win, row(ev_g_q[i]), row(ev_g_kv[i]), wuq, wuk)
            cos, sin = _rope_tables(pos_p, DH_ROPE // 2, LANES)
            qm, km, lat, kpe, qb, kbb, vbb, kb, vb = _even_project(xp, *args, cos, sin)
            olat = _mla_prompt(qm, km, b, s)
            ob = _sb_prompt(qb, kbb, vbb, b, s)
            xp = _even_output(xp, olat, ob, wuv, wout)
            p_lat.append(lat.reshape(b, s, R_KV))
            p_kpe.append(kpe.reshape(b, s, DH_ROPE))
            p_sbk.append(kb.reshape(b, s, H_B, DH_B))
            p_sbv.append(vb.reshape(b, s, H_B, DH_B))
            cos, sin = _rope_tables(pos_s, DH_ROPE // 2, LANES)
            qm, km, lat, kpe, qb, kbb, vbb, kb, vb = _even_project(xs, *args, cos, sin)
            q_rows = jnp.transpose(qm.reshape(H_A, bd, t_new, 256), (1, 2, 0, 3)).reshape(bd, t_new * H_A, 256)
            olat = _mla_sample(page_table, q_rows, km.astype(F32).reshape(bd, t_new, 256),
                               cache_mla_latent[i], cache_mla_krope[i], group=8)
            olat = jnp.transpose(olat.reshape(bd * t_new, H_A, R_KV), (1, 0, 2))
            qbd = _block_diag_rows(qb.reshape(bd, t_new, H_B * DH_B), H_B, DH_B).reshape(bd, t_new * H_B, H_B * DH_B)
            n_pool = cache_sb_k.shape[1]
            ob = _sb_sample(page_table, qbd, kb.reshape(bd, t_new, -1), vb.reshape(bd, t_new, -1),
                            cache_sb_k[i].reshape(n_pool, PAGE, H_B * DH_B),
                            cache_sb_v[i].reshape(n_pool, PAGE, H_B * DH_B), group=4)
            xs = _even_output(xs, olat, ob.reshape(bd * t_new, -1).astype(BF), wuv, wout)
            s_lat.append(lat.reshape(bd, t_new, R_KV))
            s_kpe.append(kpe.reshape(bd, t_new, DH_ROPE))
            s_sbk.append(kb.reshape(bd, t_new, H_B, DH_B))
            s_sbv.append(vb.reshape(bd, t_new, H_B, DH_B))
        else:
            win = od_w_in[i].astype(BF)
            wout = od_w_out[i].astype(BF)
            cos, sin = _rope_tables(pos_p, DH_C // 2, w_hc)
            zp = _odd_project(xp, row(mix_norm[layer]), win, cos, sin)
            cos, sin = _rope_tables(pos_s, DH_C // 2, w_hc)
            zs = _odd_project(xs, row(mix_norm[layer]), win, cos, sin)
            outs, lses = [], []
            zp3 = zp.reshape(b, s, -1)
            zs3 = zs.reshape(bd, t_new, -1)
            bufs2d = []
            for g, (window, dil) in enumerate(DIL_GROUPS):
                o, lse = _dil_prompt(zp, b, s, g, dil, window // dil)
                outs.append(o)
                lses.append(lse)
                kv = zp3[:, s - min(window, s):, (3 * g + 1) * w_hc:(3 * g + 3) * w_hc]
                p_dil[g].append(kv.reshape(b, -1, 2, H_C, DH_C))
                buf = dil_bufs[g][i]
                wc = buf.shape[1]
                bufs2d.append(buf.reshape(bd, wc // dil, dil * 2 * w_hc))
                new = zs3[:, :, (3 * g + 1) * w_hc:(3 * g + 3) * w_hc].reshape(bd, t_new, 2, H_C, DH_C)
                s_dil[g].append(jnp.concatenate([buf[:, t_new:], new], axis=1))
            xp = _odd_output(xp, outs, lses, wout)
            q_all = jnp.stack([zs3[:, :, 3 * g * w_hc:(3 * g + 1) * w_hc] for g in range(len(DIL_GROUPS))], axis=1)
            q_rows = jnp.where(
                (jnp.arange(w_hc) // DH_C)[None, :] == jnp.arange(8)[:, None], q_all[..., None, :], 0.0)
            q_rows = q_rows.reshape(bd, len(DIL_GROUPS) * t_new * 8, w_hc)
            o_s, lse_s = _dil_sample(q_rows, zs3, bufs2d)
            o_s = o_s.reshape(bd * t_new, -1)
            lse_s = lse_s.reshape(bd * t_new, -1)
            ng = len(DIL_GROUPS)
            xs = _odd_output(xs, [o_s[:, g * w_hc:(g + 1) * w_hc] for g in range(ng)],
                             [lse_s[:, g * w_hc:(g + 1) * w_hc] for g in range(ng)], wout)
        mk, mv = _mem_kv(mem, row(xa_mem_norm[layer]), xa_w_kv[layer].astype(BF))
        p_memk.append(mk.reshape(b, n_mem, H_M, DH_M))
        p_memv.append(mv.reshape(b, n_mem, H_M, DH_M))
        wq, wo = xa_w_q[layer].astype(BF), xa_w_o[layer].astype(BF)
        xp = _mem_attend(xp.reshape(b, s, d), row(xa_norm[layer]), wq, mk.reshape(b, n_mem, -1),
                         mv.reshape(b, n_mem, -1), wo).reshape(b * s, d)
        xs = _mem_attend(xs.reshape(bd, t_new, d), row(xa_norm[layer]), wq,
                         cache_mem_k[layer].reshape(bd, n_mem, -1), cache_mem_v[layer].reshape(bd, n_mem, -1),
                         wo).reshape(bd * t_new, d)
        wgu, wd = ffn_w_gu[layer, 1].astype(BF), ffn_w_down[layer, 1].astype(BF)
        last = layer == depth - 1
        xp = _ffn(xp, row(ffn_norm[layer, 1]), wgu, wd, row(final_norm), final=last)
        xs = _ffn(xs, row(ffn_norm[layer, 1]), wgu, wd, row(final_norm), final=last)

    return (xp.reshape(b, s, d), xs.reshape(bd, t_new, d),
            jnp.stack(p_lat), jnp.stack(p_kpe), jnp.stack(p_sbk), jnp.stack(p_sbv),
            jnp.stack(p_dil[0]), jnp.stack(p_dil[1]), jnp.stack(p_dil[2]),
            jnp.stack(p_memk), jnp.stack(p_memv),
            jnp.stack(s_lat), jnp.stack(s_kpe), jnp.stack(s_sbk), jnp.stack(s_sbv),
            jnp.stack(s_dil[0]), jnp.stack(s_dil[1]), jnp.stack(s_dil[2]))
```

```python
import functools
import math

import jax
import jax.numpy as jnp
from jax import lax
from jax.experimental import pallas as pl
from jax.experimental.pallas import tpu as pltpu

F32 = jnp.float32
BF = jnp.bfloat16

H_A, DH_NOPE, DH_ROPE, DH_V, R_Q, R_KV = 8, 64, 32, 64, 256, 128
H_B, DH_B = 8, 64
DIL_GROUPS = ((128, 1), (512, 4), (2048, 16))
H_C, DH_C = 4, 64
H_M, DH_M = 4, 64
PAGE = 128
ROPE_THETA = 10000.0
RMS_EPS = 1e-6
NEG_INF = -1e30
MLA_SCALE = 1.0 / math.sqrt(DH_NOPE + DH_ROPE)
SB_SCALE = 1.0 / math.sqrt(DH_B)
DIL_SCALE = 1.0 / math.sqrt(DH_C)
MEM_SCALE = 1.0 / math.sqrt(DH_M)

LANES = 128
BLK = 128
SB_DEAD = -104.0
VMEM_LIMIT = 56 * 2**20


def _cp(*sem):
    return pltpu.CompilerParams(dimension_semantics=sem, vmem_limit_bytes=VMEM_LIMIT)


def _resident(shape):
    nd = len(shape)
    return pl.BlockSpec(shape, lambda *_: (0,) * nd, pipeline_mode=pl.Buffered(1))


def _dot(a, b):
    return jnp.dot(a, b, preferred_element_type=F32)


def _dot_nt(a, b):
    return lax.dot_general(a, b, (((1,), (1,)), ((), ())), preferred_element_type=F32)


def _rms(x, g):
    return x * lax.rsqrt(jnp.mean(x * x, axis=-1, keepdims=True) + RMS_EPS) * g


def _rope(x, cos, sin_signed, half):
    n = x.shape[-1]
    lane = lax.broadcasted_iota(jnp.int32, x.shape, 1)
    first = (lane % (2 * half)) < half
    rot = jnp.where(first, pltpu.roll(x, n - half, 1), pltpu.roll(x, half, 1))
    return x * cos + rot * sin_signed


def _log_sigmoid_pair(z):
    sp = jnp.log1p(jnp.exp(-jnp.abs(z)))
    return jnp.minimum(z, 0.0) - sp, jnp.minimum(-z, 0.0) - sp


def _head_mask(shape, h, dh):
    lane = lax.broadcasted_iota(jnp.int32, shape, len(shape) - 1)
    return (lane >= h * dh) & (lane < (h + 1) * dh)


def _suffix_sum(lk, upper):
    hi = lk.astype(BF)
    lo = (lk - hi.astype(F32)).astype(BF)
    return _dot(hi, upper) + _dot(lo, upper)


def _strict_upper(n):
    r = lax.broadcasted_iota(jnp.int32, (n, n), 0)
    c = lax.broadcasted_iota(jnp.int32, (n, n), 1)
    return jnp.where(r > c, 1.0, 0.0).astype(BF)


def _ffn_body(x_ref, g_ref, wgu_ref, wd_ref, gf_ref, o_ref, a_scr, *, d_ff, chunk, final):
    x = x_ref[...]
    h = _rms(x, g_ref[...]).astype(BF)
    for c in range(d_ff // chunk):
        gate = _dot(h, wgu_ref[:, c * chunk:(c + 1) * chunk])
        up = _dot(h, wgu_ref[:, d_ff + c * chunk:d_ff + (c + 1) * chunk])
        a_scr[:, c * chunk:(c + 1) * chunk] = (gate / (1.0 + jnp.exp(-gate)) * up).astype(BF)
    y = x + 0.5 * _dot(a_scr[...], wd_ref[...])
    if final:
        y = _rms(y, gf_ref[...])
    o_ref[...] = y


def _ffn(x, g, wgu, wd, gf, *, final):
    t, d = x.shape
    f = wd.shape[0]
    tm = min(512, t)
    return pl.pallas_call(
        functools.partial(_ffn_body, d_ff=f, chunk=256, final=final),
        grid=(t // tm,),
        in_specs=[pl.BlockSpec((tm, d), lambda i: (i, 0)), _resident((1, d)), _resident((d, 2 * f)),
                  _resident((f, d)), _resident((1, d))],
        out_specs=pl.BlockSpec((tm, d), lambda i: (i, 0)),
        out_shape=jax.ShapeDtypeStruct((t, d), F32),
        scratch_shapes=[pltpu.VMEM((tm, f), BF)],
        compiler_params=_cp("parallel"), name="ffn_half")(x, g, wgu, wd, gf)


def _norm_mm_body(x_ref, g_ref, w_ref, k_ref, v_ref):
    y = _dot(_rms(x_ref[...], g_ref[...]).astype(BF), w_ref[...])
    n = k_ref.shape[-1]
    k_ref[...] = y[:, :n]
    v_ref[...] = y[:, n:]


def _mem_kv(mem, g, w):
    t, d = mem.shape
    n = w.shape[1] // 2
    tm = min(512, t)
    return pl.pallas_call(
        _norm_mm_body, grid=(t // tm,),
        in_specs=[pl.BlockSpec((tm, d), lambda i: (i, 0)), _resident((1, d)), _resident((d, 2 * n))],
        out_specs=[pl.BlockSpec((tm, n), lambda i: (i, 0))] * 2,
        out_shape=[jax.ShapeDtypeStruct((t, n), F32)] * 2,
        compiler_params=_cp("parallel"), name="mem_kv")(mem, g, w)


def _even_proj_body(x_ref, g_ref, win_ref, gq_ref, gkv_ref, wuq_ref, wuk_ref, cos_ref, sin_ref,
                    qm_ref, km_ref, lat_ref, kpe_ref, qb_ref, kbb_ref, vbb_ref, kb_ref, vb_ref):
    h = _rms(x_ref[...], g_ref[...]).astype(BF)
    z = _dot(h, win_ref[...])
    cos, sin = cos_ref[...], sin_ref[...]
    c_q = _rms(z[:, :R_Q], gq_ref[...]).astype(BF)
    c_kv = _rms(z[:, R_Q:R_Q + R_KV], gkv_ref[...])
    kpe = _rope(z[:, 384:512], cos, sin, DH_ROPE // 2)
    lat_ref[...] = c_kv
    kpe_ref[...] = kpe[:, :DH_ROPE]
    km_ref[:, :R_KV] = c_kv.astype(BF)
    km_ref[:, R_KV:] = kpe.astype(BF)
    qb_ref[...] = z[:, 512:1024].astype(BF)
    kb = z[:, 1024:1536]
    vb = z[:, 1536:2048]
    kb_ref[...] = kb
    vb_ref[...] = vb
    kbb_ref[...] = kb.astype(BF)
    vbb_ref[...] = vb.astype(BF)
    q = _dot(c_q, wuq_ref[...])
    for hp in range(H_A // 2):
        ql = _dot(q[:, hp * 128:(hp + 1) * 128].astype(BF), wuk_ref[hp])
        for e in range(2):
            hh = 2 * hp + e
            qr = _rope(q[:, 512 + hh * 128:512 + (hh + 1) * 128], cos, sin, DH_ROPE // 2)
            qm_ref[hh, :, :R_KV] = ql[:, e * 128:(e + 1) * 128].astype(BF)
            qm_ref[hh, :, R_KV:] = qr.astype(BF)


def _even_project(x, g, win, gq, gkv, wuq, wuk, cos, sin):
    t, d = x.shape
    tm = min(512, t)
    nper = cos.shape[0] // tm
    row = lambda i: (i, 0)
    outs = [((H_A, t, 256), BF, pl.BlockSpec((H_A, tm, 256), lambda i: (0, i, 0))),
            ((t, 256), BF, pl.BlockSpec((tm, 256), row)),
            ((t, R_KV), F32, pl.BlockSpec((tm, R_KV), row)),
            ((t, DH_ROPE), F32, pl.BlockSpec((tm, DH_ROPE), row)),
            ((t, 512), BF, pl.BlockSpec((tm, 512), row)),
            ((t, 512), BF, pl.BlockSpec((tm, 512), row)),
            ((t, 512), BF, pl.BlockSpec((tm, 512), row)),
            ((t, 512), F32, pl.BlockSpec((tm, 512), row)),
            ((t, 512), F32, pl.BlockSpec((tm, 512), row))]
    return pl.pallas_call(
        _even_proj_body, grid=(t // tm,),
        in_specs=[pl.BlockSpec((tm, d), row), _resident((1, d)), _resident(win.shape), _resident((1, R_Q)),
                  _resident((1, R_KV)), _resident(wuq.shape), _resident(wuk.shape),
                  pl.BlockSpec((tm, LANES), lambda i: (i % nper, 0)),
                  pl.BlockSpec((tm, LANES), lambda i: (i % nper, 0))],
        out_specs=[o[2] for o in outs],
        out_shape=[jax.ShapeDtypeStruct(o[0], o[1]) for o in outs],
        compiler_params=_cp("parallel"), name="even_project")(x, g, win, gq, gkv, wuq, wuk, cos, sin)


def _mla_prompt_body(q_ref, kv_ref, o_ref, m_scr, l_scr, acc_scr):
    i = pl.program_id(1)
    rows = H_A * BLK
    q = q_ref[...].reshape(rows, 256)
    m_scr[...] = jnp.full((rows, LANES), NEG_INF, F32)
    l_scr[...] = jnp.zeros((rows, LANES), F32)
    acc_scr[...] = jnp.zeros((rows, R_KV), F32)
    qpos = i * BLK + lax.broadcasted_iota(jnp.int32, (rows, BLK), 0) % BLK
    col = lax.broadcasted_iota(jnp.int32, (rows, BLK), 1)

    def body(j, carry):
        k = kv_ref[pl.ds(pl.multiple_of(j * BLK, BLK), BLK), :]
        s = _dot_nt(q, k) * MLA_SCALE
        s = jnp.where(j * BLK + col <= qpos, s, NEG_INF)
        m_prev = m_scr[...]
        m_new = jnp.maximum(m_prev, jnp.max(s, axis=1, keepdims=True))
        p = jnp.exp(s - m_new)
        alpha = jnp.exp(m_prev - m_new)
        l_scr[...] = alpha * l_scr[...] + jnp.sum(p, axis=1, keepdims=True)
        acc_scr[...] = alpha * acc_scr[...] + _dot(p.astype(BF), k[:, :R_KV])
        m_scr[...] = m_new
        return carry

    lax.fori_loop(0, i + 1, body, 0)
    o_ref[...] = (acc_scr[...] / l_scr[...]).reshape(H_A, BLK, R_KV).astype(BF)


def _mla_prompt(qm, km, b, s):
    nq = s // BLK
    rows = H_A * BLK
    return pl.pallas_call(
        _mla_prompt_body, grid=(b, nq),
        in_specs=[pl.BlockSpec((H_A, BLK, 256), lambda bb, i: (0, bb * nq + i, 0)),
                  pl.BlockSpec((s, 256), lambda bb, i: (bb, 0))],
        out_specs=pl.BlockSpec((H_A, BLK, R_KV), lambda bb, i: (0, bb * nq + i, 0)),
        out_shape=jax.ShapeDtypeStruct((H_A, b * s, R_KV), BF),
        scratch_shapes=[pltpu.VMEM((rows, LANES), F32), pltpu.VMEM((rows, LANES), F32),
                        pltpu.VMEM((rows, R_KV), F32)],
        compiler_params=_cp("parallel", "parallel"), name="mla_prompt")(qm, km)


def _sb_prompt_body(q_ref, k_ref, v_ref, o_ref, qe_scr, c_scr, acc_scr):
    i = pl.program_id(1)
    rows = H_B * BLK
    first = _head_mask((BLK, LANES), 0, DH_B)
    upper = _strict_upper(BLK)
    for p in range(H_B // 2):
        q = q_ref[:, p * LANES:(p + 1) * LANES].astype(F32)
        qe_scr[2 * p] = jnp.where(first, q, 0.0).astype(BF)
        qe_scr[2 * p + 1] = jnp.where(first, 0.0, q).astype(BF)
    c_scr[...] = jnp.zeros(c_scr.shape, F32)
    acc_scr[...] = jnp.zeros(acc_scr.shape, F32)

    def fold(j, diagonal):
        off = pl.multiple_of(j * BLK, BLK)
        z = jnp.concatenate(
            [_dot_nt(qe_scr[h], k_ref[pl.ds(off, BLK), (h // 2) * LANES:(h // 2 + 1) * LANES]) for h in range(H_B)],
            axis=0) * SB_SCALE
        ls_pos, ls_neg = _log_sigmoid_pair(z)
        if diagonal:
            valid = (lax.broadcasted_iota(jnp.int32, (rows, BLK), 1)
                     < lax.broadcasted_iota(jnp.int32, (rows, BLK), 0) % BLK)
            ls_neg = jnp.where(valid, ls_neg, 0.0)
        c = c_scr[...]
        w = jnp.exp(ls_pos + _suffix_sum(ls_neg, upper) + c)
        if diagonal:
            w = jnp.where(valid, w, 0.0)
        w = w.astype(BF)
        for p in range(H_B // 2):
            v = v_ref[pl.ds(off, BLK), p * LANES:(p + 1) * LANES]
            d0 = _dot(w[2 * p * BLK:(2 * p + 1) * BLK], v)
            d1 = _dot(w[(2 * p + 1) * BLK:(2 * p + 2) * BLK], v)
            acc_scr[:, p * LANES:(p + 1) * LANES] += jnp.where(first, d0, d1)
        c = c + jnp.sum(ls_neg, axis=1, keepdims=True)
        c_scr[...] = c
        return (jnp.max(c) > SB_DEAD).astype(jnp.int32)

    alive = fold(i, True)
    lax.while_loop(lambda st: (st[0] >= 0) & (st[1] > 0),
                   lambda st: (st[0] - 1, fold(st[0], False)), (i - 1, alive))
    o_ref[...] = acc_scr[...].astype(BF)


def _sb_prompt(qb, kb, vb, b, s):
    nq = s // BLK
    w = H_B * DH_B
    return pl.pallas_call(
        _sb_prompt_body, grid=(b, nq),
        in_specs=[pl.BlockSpec((BLK, w), lambda bb, i: (bb * nq + i, 0)),
                  pl.BlockSpec((s, w), lambda bb, i: (bb, 0)),
                  pl.BlockSpec((s, w), lambda bb, i: (bb, 0))],
        out_specs=pl.BlockSpec((BLK, w), lambda bb, i: (bb * nq + i, 0)),
        out_shape=jax.ShapeDtypeStruct((b * s, w), BF),
        scratch_shapes=[pltpu.VMEM((H_B, BLK, LANES), BF), pltpu.VMEM((H_B * BLK, 1), F32),
                        pltpu.VMEM((BLK, w), F32)],
        compiler_params=_cp("parallel", "parallel"), name="sb_prompt")(qb, kb, vb)


def _even_out_body(x_ref, ol_ref, ob_ref, wuv_ref, wout_ref, o_ref):
    parts = []
    for hp in range(H_A // 2):
        ol = jnp.concatenate([ol_ref[2 * hp], ol_ref[2 * hp + 1]], axis=1)
        parts.append(_dot(ol, wuv_ref[hp]).astype(BF))
    oa = jnp.concatenate(parts, axis=1)
    na = H_A * DH_V
    o_ref[...] = x_ref[...] + _dot(oa, wout_ref[:na, :]) + _dot(ob_ref[...], wout_ref[na:, :])


def _even_output(x, olat, ob, wuv, wout):
    t, d = x.shape
    tm = min(512, t)
    row = lambda i: (i, 0)
    return pl.pallas_call(
        _even_out_body, grid=(t // tm,),
        in_specs=[pl.BlockSpec((tm, d), row), pl.BlockSpec((H_A, tm, R_KV), lambda i: (0, i, 0)),
                  pl.BlockSpec((tm, 512), row), _resident(wuv.shape), _resident(wout.shape)],
        out_specs=pl.BlockSpec((tm, d), row),
        out_shape=jax.ShapeDtypeStruct((t, d), F32),
        compiler_params=_cp("parallel"), name="even_output")(x, olat, ob, wuv, wout)


def _mla_sample_body(pt_ref, q_ref, new_ref, *rest, group, t_new):
    lat_refs = rest[:group]
    kr_refs = rest[group:2 * group]
    o_ref, m_scr, l_scr, acc_scr, lat_scr, kr_scr = rest[2 * group:]
    p = pl.program_id(1)
    rows = q_ref.shape[1]

    @pl.when(p == 0)
    def _():
        m_scr[...] = jnp.full((rows, LANES), NEG_INF, F32)
        l_scr[...] = jnp.zeros((rows, LANES), F32)
        acc_scr[...] = jnp.zeros((rows, R_KV), F32)

    q = q_ref[0]
    ql = q[:, :R_KV]
    qr = q[:, R_KV:R_KV + DH_ROPE]
    for g in range(group):
        lat_scr[g * PAGE:(g + 1) * PAGE, :] = lat_refs[g][0].astype(BF)
        kr_scr[:, g * PAGE:(g + 1) * PAGE] = kr_refs[g][0].astype(BF)
    lat = lat_scr[...]
    s = (_dot_nt(ql, lat) + _dot(qr, kr_scr[...])) * MLA_SCALE
    m_prev = m_scr[...]
    m_new = jnp.maximum(m_prev, jnp.max(s, axis=1, keepdims=True))
    pr = jnp.exp(s - m_new[:, :1])
    alpha = jnp.exp(m_prev - m_new)
    l_scr[...] = alpha * l_scr[...] + jnp.sum(pr, axis=1, keepdims=True)
    acc_scr[...] = alpha * acc_scr[...] + _dot(pr.astype(BF), lat)
    m_scr[...] = m_new

    @pl.when(p == pl.num_programs(1) - 1)
    def _():
        qf = q.astype(F32)
        tq = lax.broadcasted_iota(jnp.int32, (rows, 1), 0) // H_A
        m = m_scr[...]
        l = l_scr[...]
        acc = acc_scr[...]
        for j in range(t_new):
            kn = new_ref[0, j:j + 1, :].astype(F32)
            s = jnp.sum(qf * kn, axis=1, keepdims=True) * MLA_SCALE
            s = jnp.where(j <= tq, s, NEG_INF)
            m_new = jnp.maximum(m, s)
            pj = jnp.exp(s - m_new)
            alpha = jnp.exp(m - m_new)
            l = alpha * l + pj
            acc = alpha * acc + pj.astype(BF).astype(F32) * kn[:, :R_KV]
            m = m_new
        o_ref[0] = (acc / l).astype(BF)


def _mla_sample(page_table, q, new, lat_pool, kr_pool, group):
    bd, n_pages = page_table.shape
    rows = q.shape[1]
    t_new = new.shape[1]
    lat_specs = [pl.BlockSpec((1, PAGE, R_KV), functools.partial(
        lambda b, p, pt, g: (pt[b, p * group + g], 0, 0), g=g)) for g in range(group)]
    kr_specs = [pl.BlockSpec((1, DH_ROPE, PAGE), functools.partial(
        lambda b, p, pt, g: (pt[b, p * group + g], 0, 0), g=g)) for g in range(group)]
    grid_spec = pltpu.PrefetchScalarGridSpec(
        num_scalar_prefetch=1, grid=(bd, n_pages // group),
        in_specs=[pl.BlockSpec((1, rows, 256), lambda b, p, pt: (b, 0, 0)),
                  pl.BlockSpec((1, t_new, 256), lambda b, p, pt: (b, 0, 0))] + lat_specs + kr_specs,
        out_specs=pl.BlockSpec((1, rows, R_KV), lambda b, p, pt: (b, 0, 0)),
        scratch_shapes=[pltpu.VMEM((rows, LANES), F32), pltpu.VMEM((rows, LANES), F32),
                        pltpu.VMEM((rows, R_KV), F32), pltpu.VMEM((group * PAGE, R_KV), BF),
                        pltpu.VMEM((DH_ROPE, group * PAGE), BF)])
    return pl.pallas_call(
        functools.partial(_mla_sample_body, group=group, t_new=t_new), grid_spec=grid_spec,
        out_shape=jax.ShapeDtypeStruct((bd, rows, R_KV), BF),
        compiler_params=_cp("parallel", "arbitrary"), name="mla_sample")(
            page_table, q, new, *([lat_pool] * group), *([kr_pool] * group))


def _sb_sample_body(pt_ref, q_ref, kn_ref, vn_ref, k_hbm, v_hbm, o_ref, kbuf, vbuf, sem, c_scr, acc_scr,
                    *, layer, n_pages, t_new):
    b = pl.program_id(0)
    rows = H_B * 8
    slot_t = lax.broadcasted_iota(jnp.int32, (rows, 1), 0) % 8
    real = slot_t < t_new

    def copies(p, slot):
        pg = pt_ref[b, p]
        return (pltpu.make_async_copy(k_hbm.at[layer, pg], kbuf.at[slot], sem.at[0, slot]),
                pltpu.make_async_copy(v_hbm.at[layer, pg], vbuf.at[slot], sem.at[1, slot]))

    def start(p, slot):
        for cp in copies(p, slot):
            cp.start()

    def wait(p, slot):
        for cp in copies(p, slot):
            cp.wait()

    start(n_pages - 1, 0)

    q = q_ref[0]
    qf = q.astype(BF).astype(F32)
    q3 = q.reshape(H_B, 8, DH_B).astype(BF)
    c = jnp.zeros((rows, 1), F32)
    acc = jnp.zeros((rows, DH_B), F32)
    for j in reversed(range(t_new)):
        kn = kn_ref[0, j].astype(BF).astype(F32)
        vn = vn_ref[0, j].astype(BF).astype(F32)
        z = jnp.sum(qf * kn, axis=1, keepdims=True) * SB_SCALE
        ls_pos, ls_neg = _log_sigmoid_pair(z)
        valid = j < slot_t
        w = jnp.where(valid, jnp.exp(ls_pos + c), 0.0)
        acc = acc + w.astype(BF).astype(F32) * vn
        c = c + jnp.where(valid, ls_neg, 0.0)
    c_scr[...] = c
    acc_scr[...] = acc
    upper = _strict_upper(PAGE)

    def alive_of(cc):
        return (jnp.max(jnp.where(real, cc, NEG_INF)) > SB_DEAD).astype(jnp.int32)

    def body(st):
        p, _ = st
        slot = lax.rem(n_pages - 1 - p, 2)
        wait(p, slot)

        @pl.when(p >= 1)
        def _():
            start(p - 1, 1 - slot)

        kt = kbuf[slot].astype(BF)
        vt = vbuf[slot].astype(BF)
        z = lax.dot_general(q3, kt, (((2,), (1,)), ((0,), (0,))), preferred_element_type=F32)
        z = z.reshape(rows, PAGE) * SB_SCALE
        ls_pos, ls_neg = _log_sigmoid_pair(z)
        cc = c_scr[...]
        w = jnp.exp(ls_pos + _suffix_sum(ls_neg, upper) + cc).reshape(H_B, 8, PAGE).astype(BF)
        d = lax.dot_general(w, vt, (((2,), (2,)), ((0,), (0,))), preferred_element_type=F32)
        acc_scr[...] = acc_scr[...] + d.reshape(rows, DH_B)
        cc = cc + jnp.sum(ls_neg, axis=1, keepdims=True)
        c_scr[...] = cc
        return p - 1, alive_of(cc)

    p_next, _ = lax.while_loop(lambda st: (st[0] >= 0) & (st[1] > 0), body, (n_pages - 1, alive_of(c)))

    @pl.when(p_next >= 0)
    def _():
        wait(p_next, lax.rem(n_pages - 1 - p_next, 2))

    o_ref[0] = acc_scr[...]


def _sb_sample(page_table, q, k_new, v_new, k_pool, v_pool, layer):
    bd, n_pages = page_table.shape
    t_new = q.shape[1]
    rows = H_B * 8
    q_rows = jnp.pad(jnp.transpose(q, (0, 2, 1, 3)), ((0, 0), (0, 0), (0, 8 - t_new), (0, 0))).reshape(bd, rows, DH_B)
    spread = lambda a: jnp.broadcast_to(a[:, :, :, None, :], (bd, t_new, H_B, 8, DH_B)).reshape(bd, t_new, rows, DH_B)
    own3 = lambda b, pt: (b, 0, 0)
    own4 = lambda b, pt: (b, 0, 0, 0)
    grid_spec = pltpu.PrefetchScalarGridSpec(
        num_scalar_prefetch=1, grid=(bd,),
        in_specs=[pl.BlockSpec((1, rows, DH_B), own3), pl.BlockSpec((1, t_new, rows, DH_B), own4),
                  pl.BlockSpec((1, t_new, rows, DH_B), own4),
                  pl.BlockSpec(memory_space=pl.ANY), pl.BlockSpec(memory_space=pl.ANY)],
        out_specs=pl.BlockSpec((1, rows, DH_B), own3),
        scratch_shapes=[pltpu.VMEM((2, H_B, DH_B, PAGE), F32), pltpu.VMEM((2, H_B, DH_B, PAGE), F32),
                        pltpu.SemaphoreType.DMA((2, 2)), pltpu.VMEM((rows, 1), F32), pltpu.VMEM((rows, DH_B), F32)])
    by_head = lambda pool: jnp.transpose(pool, (0, 1, 3, 4, 2))
    out = pl.pallas_call(
        functools.partial(_sb_sample_body, layer=layer, n_pages=n_pages, t_new=t_new), grid_spec=grid_spec,
        out_shape=jax.ShapeDtypeStruct((bd, rows, DH_B), F32),
        compiler_params=_cp("arbitrary"), name="sb_sample")(
            page_table, q_rows, spread(k_new), spread(v_new), by_head(k_pool), by_head(v_pool))
    return jnp.transpose(out.reshape(bd, H_B, 8, DH_B)[:, :, :t_new], (0, 2, 1, 3)).reshape(bd * t_new, H_B * DH_B)


def _odd_proj_body(x_ref, g_ref, win_ref, cos_ref, sin_ref, z_ref):
    h = _rms(x_ref[...], g_ref[...]).astype(BF)
    z = _dot(h, win_ref[...])
    cos, sin = cos_ref[...], sin_ref[...]
    w = H_C * DH_C
    for c in range(z.shape[1] // w):
        part = z[:, c * w:(c + 1) * w]
        if c % 3 != 2:
            part = _rope(part, cos, sin, DH_C // 2)
        z_ref[:, c * w:(c + 1) * w] = part


def _odd_project(x, g, win, cos, sin):
    t, d = x.shape
    n = win.shape[1]
    tm = min(512, t)
    nper = cos.shape[0] // tm
    w = H_C * DH_C
    return pl.pallas_call(
        _odd_proj_body, grid=(t // tm,),
        in_specs=[pl.BlockSpec((tm, d), lambda i: (i, 0)), _resident((1, d)), _resident(win.shape),
                  pl.BlockSpec((tm, w), lambda i: (i % nper, 0)), pl.BlockSpec((tm, w), lambda i: (i % nper, 0))],
        out_specs=pl.BlockSpec((tm, n), lambda i: (i, 0)),
        out_shape=jax.ShapeDtypeStruct((t, n), F32),
        compiler_params=_cp("parallel"), name="odd_project")(x, g, win, cos, sin)


def _dil_prompt_body(q0_ref, q1_ref, k0_ref, k1_ref, v0_ref, v1_ref, o_ref, lse_ref, *, dil, nback):
    s_len = q0_ref.shape[0]
    w = H_C * DH_C
    win = BLK * dil
    both = lambda r0, r1, rows: jnp.concatenate([r0[rows, :], r1[rows, :]], axis=1)
    row = lax.broadcasted_iota(jnp.int32, (H_C * BLK, 2 * BLK), 0) % BLK
    col = lax.broadcasted_iota(jnp.int32, (H_C * BLK, 2 * BLK), 1)
    diff = row + BLK - col
    band = (diff >= 0) & (diff <= nback)
    masks = [_head_mask((BLK, w), h, DH_C) for h in range(H_C)]

    def block(jb, carry):
        base = pl.multiple_of(jb * win, win)
        prev = pl.multiple_of(jnp.maximum(jb - 1, 0) * win, win)
        first_col = jnp.where(jb > 0, 0, BLK)
        mask = band & (col >= first_col)
        for r in range(dil):
            cur = pl.ds(base + r, BLK, stride=dil)
            old = pl.ds(prev + r, BLK, stride=dil)
            q = both(q0_ref, q1_ref, cur)
            kk = jnp.concatenate([both(k0_ref, k1_ref, old), both(k0_ref, k1_ref, cur)], axis=0).astype(BF)
            vv = jnp.concatenate([both(v0_ref, v1_ref, old), both(v0_ref, v1_ref, cur)], axis=0).astype(BF)
            s = jnp.concatenate([_dot_nt(jnp.where(masks[h], q, 0.0).astype(BF), kk) for h in range(H_C)], axis=0)
            s = jnp.where(mask, s * DIL_SCALE, NEG_INF)
            m = jnp.max(s, axis=1, keepdims=True)
            p = jnp.exp(s - m)
            l = jnp.sum(p, axis=1, keepdims=True)
            pb = p.astype(BF)
            lse_rows = m + jnp.log(l)
            o = jnp.zeros((BLK, w), F32)
            lse = jnp.zeros((BLK, w), F32)
            for h in range(H_C):
                rows_h = slice(h * BLK, (h + 1) * BLK)
                o = jnp.where(masks[h], _dot(pb[rows_h], vv) / l[rows_h], o)
                lse = jnp.where(masks[h], lse_rows[rows_h], lse)
            for half in range(2):
                o_ref[half, cur, :] = o[:, half * LANES:(half + 1) * LANES]
                lse_ref[half, cur, :] = lse[:, half * LANES:(half + 1) * LANES]
        return carry

    lax.fori_loop(0, s_len // win, block, 0)


def _dil_prompt(zr, b, s, g, dil, nback):
    spec = lambda part, half: pl.BlockSpec((s, LANES), lambda bb: (bb, 2 * (3 * g + part) + half))
    out_spec = pl.BlockSpec((2, s, LANES), lambda bb: (0, bb, 0))
    return pl.pallas_call(
        functools.partial(_dil_prompt_body, dil=dil, nback=nback), grid=(b,),
        in_specs=[spec(part, half) for part in range(3) for half in range(2)],
        out_specs=[out_spec, out_spec],
        out_shape=[jax.ShapeDtypeStruct((2, b * s, LANES), F32)] * 2,
        compiler_params=_cp("parallel"), name=f"dil_prompt_{g}")(*([zr] * 6))


def _dil_sample_body(q_ref, z_ref, b0_ref, b1_ref, b2_ref, o_ref, lse_ref, *, t_new):
    w = H_C * DH_C
    rows = H_C * 8
    bufs = (b0_ref, b1_ref, b2_ref)
    slot = lax.broadcasted_iota(jnp.int32, (rows, 1), 0) % 8
    pick = (lax.broadcasted_iota(jnp.int32, (rows, w), 0) // 8
            == lax.broadcasted_iota(jnp.int32, (rows, w), 1) // DH_C)
    fold = lambda a: functools.reduce(lambda x, y: x + y, [a[h * 8:(h + 1) * 8] for h in range(H_C)])
    for g, (window, dil) in enumerate(DIL_GROUPS):
        q = q_ref[0, g * rows:(g + 1) * rows, :].astype(BF)
        qf = q.astype(F32)
        knew = z_ref[0, :, (3 * g + 1) * w:(3 * g + 2) * w].astype(BF).astype(F32)
        vnew = z_ref[0, :, (3 * g + 2) * w:(3 * g + 3) * w].astype(BF).astype(F32)
        kt = bufs[g][0, 0].reshape(w, window).astype(BF)
        vt = bufs[g][0, 1].reshape(w, window).astype(BF)
        col = lax.broadcasted_iota(jnp.int32, (rows, window), 1)
        valid = (col >= slot) & (col % dil == slot % dil)
        s_buf = jnp.where(valid, _dot(q, kt) * DIL_SCALE, NEG_INF)
        s_new = []
        for j in range(t_new):
            sj = jnp.sum(qf * knew[j:j + 1, :], axis=1, keepdims=True) * DIL_SCALE
            s_new.append(jnp.where((j <= slot) & ((slot - j) % dil == 0), sj, NEG_INF))
        m = functools.reduce(jnp.maximum, s_new, jnp.max(s_buf, axis=1, keepdims=True))
        p_buf = jnp.exp(s_buf - m)
        l = jnp.sum(p_buf, axis=1, keepdims=True)
        o = _dot_nt(p_buf.astype(BF), vt)
        for j, sj in enumerate(s_new):
            pj = jnp.exp(sj - m)
            l = l + pj
            o = o + pj.astype(BF).astype(F32) * vnew[j:j + 1, :]
        o = jnp.where(pick, o / l, 0.0)
        lse = jnp.where(pick, m + jnp.log(l), 0.0)
        o_ref[0, :, g * w:(g + 1) * w] = fold(o)[:t_new]
        lse_ref[0, :, g * w:(g + 1) * w] = fold(lse)[:t_new]


def _dil_sample(q_rows, zr, bufs):
    bd, t_new, n = zr.shape
    w = H_C * DH_C
    own = lambda b: (b, 0, 0)
    views = [jnp.transpose(buf, (0, 2, 3, 4, 1)) for buf in bufs]
    buf_specs = [pl.BlockSpec((1,) + v.shape[1:], lambda b: (b, 0, 0, 0, 0)) for v in views]
    ng = len(DIL_GROUPS)
    return pl.pallas_call(
        functools.partial(_dil_sample_body, t_new=t_new), grid=(bd,),
        in_specs=[pl.BlockSpec((1, q_rows.shape[1], w), own), pl.BlockSpec((1, t_new, n), own)] + buf_specs,
        out_specs=[pl.BlockSpec((1, t_new, ng * w), own)] * 2,
        out_shape=[jax.ShapeDtypeStruct((bd, t_new, ng * w), F32)] * 2,
        compiler_params=_cp("parallel"), name="dil_sample")(q_rows, zr, *views)


def _odd_out_body(x_ref, *rest):
    ng = len(DIL_GROUPS)
    o_refs, l_refs = rest[:ng], rest[ng:2 * ng]
    w_ref, out_ref = rest[2 * ng:]
    whole = lambda r: jnp.concatenate([r[0], r[1]], axis=1)
    ls = [whole(r) for r in l_refs]
    m = functools.reduce(jnp.maximum, ls)
    es = [jnp.exp(l - m) for l in ls]
    den = functools.reduce(lambda a, b: a + b, es)
    mix = functools.reduce(lambda a, b: a + b, [e / den * whole(r) for e, r in zip(es, o_refs)])
    out_ref[...] = x_ref[...] + _dot(mix.astype(BF), w_ref[...])


def _odd_output(x, outs, lses, wout):
    t, d = x.shape
    tm = min(512, t)
    row = lambda i: (i, 0)
    halves = pl.BlockSpec((2, tm, LANES), lambda i: (0, i, 0))
    return pl.pallas_call(
        _odd_out_body, grid=(t // tm,),
        in_specs=[pl.BlockSpec((tm, d), row)] + [halves] * (2 * len(outs)) + [_resident(wout.shape)],
        out_specs=pl.BlockSpec((tm, d), row),
        out_shape=jax.ShapeDtypeStruct((t, d), F32),
        compiler_params=_cp("parallel"), name="odd_output")(x, *outs, *lses, wout)


def _mem_attn_body(x_ref, g_ref, wq_ref, k_ref, v_ref, wo_ref, o_ref):
    x = x_ref[0]
    w = H_M * DH_M
    q = _dot(_rms(x, g_ref[...]).astype(BF), wq_ref[...])
    kk = k_ref[0].astype(BF)
    vv = v_ref[0].astype(BF)
    o = jnp.zeros((x.shape[0], w), F32)
    for h in range(H_M):
        hm = _head_mask(q.shape, h, DH_M)
        s = _dot_nt(jnp.where(hm, q, 0.0).astype(BF), kk) * MEM_SCALE
        m = jnp.max(s, axis=1, keepdims=True)
        p = jnp.exp(s - m)
        l = jnp.sum(p, axis=1, keepdims=True)
        o = jnp.where(hm, _dot(p.astype(BF), vv) / l, o)
    o_ref[0] = x + _dot(o.astype(BF), wo_ref[...])


def _mem_attend(x, g, wq, k, v, wo):
    b, s, d = x.shape
    n_mem, w = k.shape[1:]
    tm = min(512, s)
    return pl.pallas_call(
        _mem_attn_body, grid=(b, s // tm),
        in_specs=[pl.BlockSpec((1, tm, d), lambda bb, i: (bb, i, 0)), _resident((1, d)), _resident(wq.shape),
                  pl.BlockSpec((1, n_mem, w), lambda bb, i: (bb, 0, 0)),
                  pl.BlockSpec((1, n_mem, w), lambda bb, i: (bb, 0, 0)), _resident(wo.shape)],
        out_specs=pl.BlockSpec((1, tm, d), lambda bb, i: (bb, i, 0)),
        out_shape=jax.ShapeDtypeStruct((b, s, d), F32),
        compiler_params=_cp("parallel", "parallel"), name="mem_attend")(x, g, wq, k, v, wo)


def _rope_tables(pos, half, width):
    inv_freq = ROPE_THETA ** (-jnp.arange(half, dtype=F32) / half)
    ang = pos.astype(F32)[:, None] * inv_freq[None, :]
    cos, sin = jnp.cos(ang), jnp.sin(ang)
    reps = width // (2 * half)
    return (jnp.tile(jnp.concatenate([cos, cos], axis=1), (1, reps)),
            jnp.tile(jnp.concatenate([-sin, sin], axis=1), (1, reps)))


def _even_weights(w_in, w_uq, w_uk, w_uv):
    d = w_in.shape[0]
    c0 = R_Q + R_KV
    win = jnp.concatenate([w_in[:, :c0], w_in[:, c0:c0 + DH_ROPE], jnp.zeros((d, LANES - DH_ROPE), F32),
                           w_in[:, c0 + DH_ROPE:]], axis=1).astype(BF)
    nope = w_uq[:, :, :DH_NOPE].reshape(R_Q, H_A * DH_NOPE)
    rope = jnp.pad(w_uq[:, :, DH_NOPE:], ((0, 0), (0, 0), (0, LANES - DH_ROPE))).reshape(R_Q, H_A * LANES)
    wuq = jnp.concatenate([nope, rope], axis=1).astype(BF)
    uk = jnp.transpose(w_uk, (1, 2, 0)).reshape(H_A // 2, 2, DH_NOPE, R_KV)
    z = jnp.zeros_like(uk[:, 0])
    wuk = jnp.concatenate([jnp.concatenate([uk[:, 0], z], axis=2),
                           jnp.concatenate([z, uk[:, 1]], axis=2)], axis=1).astype(BF)
    uv = jnp.transpose(w_uv, (1, 0, 2)).reshape(H_A // 2, 2, R_KV, DH_V)
    zv = jnp.zeros_like(uv[:, 0])
    wuv = jnp.concatenate([jnp.concatenate([uv[:, 0], zv], axis=2),
                           jnp.concatenate([zv, uv[:, 1]], axis=2)], axis=1).astype(BF)
    return win, wuq, wuk, wuv


def kernel(x_prompt, x_sample, mem_prompt, cache_mla_latent, cache_mla_krope, cache_sb_k, cache_sb_v,
           cache_dil_w128, cache_dil_w512, cache_dil_w2048, cache_mem_k, cache_mem_v, page_table,
           ffn_norm, ffn_w_gu, ffn_w_down, mix_norm,
           ev_w_in, ev_g_q, ev_g_kv, ev_w_uq, ev_w_uk, ev_w_uv, ev_w_out,
           od_w_in, od_w_out,
           xa_norm, xa_mem_norm, xa_w_q, xa_w_kv, xa_w_o, final_norm):
    b, s, d = x_prompt.shape
    bd, t_new, _ = x_sample.shape
    n_mem = mem_prompt.shape[1]
    depth = mix_norm.shape[0]
    n_pages = page_table.shape[1]
    past = n_pages * PAGE
    pos_p = jnp.arange(s, dtype=jnp.int32)
    pos_s = jnp.tile(past + jnp.arange(t_new, dtype=jnp.int32), bd)
    dil_bufs = (cache_dil_w128, cache_dil_w512, cache_dil_w2048)
    w_hc = H_C * DH_C

    xp = x_prompt.reshape(b * s, d)
    xs = x_sample.reshape(bd * t_new, d)
    mem = mem_prompt.reshape(b * n_mem, d)
    row = lambda v: v.reshape(1, -1)

    p_lat, p_kpe, p_sbk, p_sbv, s_lat, s_kpe, s_sbk, s_sbv = ([] for _ in range(8))
    p_dil = [[] for _ in DIL_GROUPS]
    s_dil = [[] for _ in DIL_GROUPS]
    p_memk, p_memv = [], []

    for layer in range(depth):
        i = layer // 2
        wgu, wd = ffn_w_gu[layer, 0].astype(BF), ffn_w_down[layer, 0].astype(BF)
        xp = _ffn(xp, row(ffn_norm[layer, 0]), wgu, wd, row(final_norm), final=False)
        xs = _ffn(xs, row(ffn_norm[layer, 0]), wgu, wd, row(final_norm), final=False)
        if layer % 2 == 0:
            win, wuq, wuk, wuv = _even_weights(ev_w_in[i], ev_w_uq[i], ev_w_uk[i], ev_w_uv[i])
            wout = ev_w_out[i].astype(BF)
            args = (row(mix_norm[layer]), win, row(ev_g_q[i]), row(ev_g_kv[i]), wuq, wuk)
            cos, sin = _rope_tables(pos_p, DH_ROPE // 2, LANES)
            qm, km, lat, kpe, qb, kbb, vbb, kb, vb = _even_project(xp, *args, cos, sin)
            olat = _mla_prompt(qm, km, b, s)
            ob = _sb_prompt(qb, kbb, vbb, b, s)
            xp = _even_output(xp, olat, ob, wuv, wout)
            p_lat.append(lat.reshape(b, s, R_KV))
            p_kpe.append(kpe.reshape(b, s, DH_ROPE))
            p_sbk.append(kb.reshape(b, s, H_B, DH_B))
            p_sbv.append(vb.reshape(b, s, H_B, DH_B))
            cos, sin = _rope_tables(pos_s, DH_ROPE // 2, LANES)
            qm, km, lat, kpe, qb, kbb, vbb, kb, vb = _even_project(xs, *args, cos, sin)
            q_rows = jnp.transpose(qm.reshape(H_A, bd, t_new, 256), (1, 2, 0, 3)).reshape(bd, t_new * H_A, 256)
            olat = _mla_sample(page_table, q_rows, km.astype(F32).reshape(bd, t_new, 256),
                               cache_mla_latent[i], jnp.swapaxes(cache_mla_krope[i], 1, 2),
                               group=min(16, n_pages))
            olat = jnp.transpose(olat.reshape(bd * t_new, H_A, R_KV), (1, 0, 2))
            heads = lambda a: a.astype(F32).reshape(bd, t_new, H_B, DH_B)
            ob = _sb_sample(page_table, heads(qb), heads(kb), heads(vb), cache_sb_k, cache_sb_v, layer=i)
            xs = _even_output(xs, olat, ob.astype(BF), wuv, wout)
            s_lat.append(lat.reshape(bd, t_new, R_KV))
            s_kpe.append(kpe.reshape(bd, t_new, DH_ROPE))
            s_sbk.append(kb.reshape(bd, t_new, H_B, DH_B))
            s_sbv.append(vb.reshape(bd, t_new, H_B, DH_B))
        else:
            win = od_w_in[i].astype(BF)
            wout = od_w_out[i].astype(BF)
            cos, sin = _rope_tables(pos_p, DH_C // 2, w_hc)
            zp = _odd_project(xp, row(mix_norm[layer]), win, cos, sin)
            cos, sin = _rope_tables(pos_s, DH_C // 2, w_hc)
            zs = _odd_project(xs, row(mix_norm[layer]), win, cos, sin)
            outs, lses = [], []
            zp3 = zp.reshape(b, s, -1)
            zs3 = zs.reshape(bd, t_new, -1)
            bufs = []
            for g, (window, dil) in enumerate(DIL_GROUPS):
                o, lse = _dil_prompt(zp, b, s, g, dil, window // dil)
                outs.append(o)
                lses.append(lse)
                kv = zp3[:, s - min(window, s):, (3 * g + 1) * w_hc:(3 * g + 3) * w_hc]
                p_dil[g].append(kv.reshape(b, -1, 2, H_C, DH_C))
                buf = dil_bufs[g][i]
                assert buf.shape[1] == window and t_new <= 8
                bufs.append(buf)
                new = zs3[:, :, (3 * g + 1) * w_hc:(3 * g + 3) * w_hc].reshape(bd, t_new, 2, H_C, DH_C)
                s_dil[g].append(jnp.concatenate([buf[:, t_new:], new], axis=1))
            xp = _odd_output(xp, outs, lses, wout)
            ng = len(DIL_GROUPS)
            q_all = jnp.stack([zs3[:, :, 3 * g * w_hc:(3 * g + 1) * w_hc] for g in range(ng)], axis=1)
            q_all = jnp.pad(q_all, ((0, 0), (0, 0), (0, 8 - t_new), (0, 0)))
            q_rows = jnp.where(
                (jnp.arange(w_hc) // DH_C)[None, None, :] == jnp.arange(H_C)[:, None, None],
                q_all[:, :, None, :, :], 0.0).reshape(bd, ng * H_C * 8, w_hc)
            o_s, lse_s = _dil_sample(q_rows, zs3, bufs)
            halves = lambda a: jnp.transpose(a.reshape(bd * t_new, ng, 2, LANES), (1, 2, 0, 3))
            o_s, lse_s = halves(o_s), halves(lse_s)
            xs = _odd_output(xs, [o_s[g] for g in range(ng)], [lse_s[g] for g in range(ng)], wout)
        mk, mv = _mem_kv(mem, row(xa_mem_norm[layer]), xa_w_kv[layer].astype(BF))
        p_memk.append(mk.reshape(b, n_mem, H_M, DH_M))
        p_memv.append(mv.reshape(b, n_mem, H_M, DH_M))
        wq, wo = xa_w_q[layer].astype(BF), xa_w_o[layer].astype(BF)
        xp = _mem_attend(xp.reshape(b, s, d), row(xa_norm[layer]), wq, mk.reshape(b, n_mem, -1),
                         mv.reshape(b, n_mem, -1), wo).reshape(b * s, d)
        xs = _mem_attend(xs.reshape(bd, t_new, d), row(xa_norm[layer]), wq,
                         cache_mem_k[layer].reshape(bd, n_mem, -1), cache_mem_v[layer].reshape(bd, n_mem, -1),
                         wo).reshape(bd * t_new, d)
        wgu, wd = ffn_w_gu[layer, 1].astype(BF), ffn_w_down[layer, 1].astype(BF)
        last = layer == depth - 1
        xp = _ffn(xp, row(ffn_norm[layer, 1]), wgu, wd, row(final_norm), final=last)
        xs = _ffn(xs, row(ffn_norm[layer, 1]), wgu, wd, row(final_norm), final=last)

    return (xp.reshape(b, s, d), xs.reshape(bd, t_new, d),
            jnp.stack(p_lat), jnp.stack(p_kpe), jnp.stack(p_sbk), jnp.stack(p_sbv),
            jnp.stack(p_dil[0]), jnp.stack(p_dil[1]), jnp.stack(p_dil[2]),
            jnp.stack(p_memk), jnp.stack(p_memv),
            jnp.stack(s_lat), jnp.stack(s_kpe), jnp.stack(s_sbk), jnp.stack(s_sbv),
            jnp.stack(s_dil[0]), jnp.stack(s_dil[1]), jnp.stack(s_dil[2]))
```

```python
import functools
import math

import jax
import jax.numpy as jnp
from jax import lax
from jax.experimental import pallas as pl
from jax.experimental.pallas import tpu as pltpu

F32 = jnp.float32
BF = jnp.bfloat16

H_A, DH_NOPE, DH_ROPE, DH_V, R_Q, R_KV = 8, 64, 32, 64, 256, 128
H_B, DH_B = 8, 64
DIL_GROUPS = ((128, 1), (512, 4), (2048, 16))
H_C, DH_C = 4, 64
H_M, DH_M = 4, 64
PAGE = 128
ROPE_THETA = 10000.0
RMS_EPS = 1e-6
NEG_INF = -1e30
MLA_SCALE = 1.0 / math.sqrt(DH_NOPE + DH_ROPE)
SB_SCALE = 1.0 / math.sqrt(DH_B)
DIL_SCALE = 1.0 / math.sqrt(DH_C)
MEM_SCALE = 1.0 / math.sqrt(DH_M)

LANES = 128
BLK = 128
SB_DEAD = -104.0
VMEM_LIMIT = 56 * 2**20


def _cp(*sem):
    return pltpu.CompilerParams(dimension_semantics=sem, vmem_limit_bytes=VMEM_LIMIT)


def _resident(shape):
    nd = len(shape)
    return pl.BlockSpec(shape, lambda *_: (0,) * nd, pipeline_mode=pl.Buffered(1))


def _dot(a, b):
    return jnp.dot(a, b, preferred_element_type=F32)


def _dot_nt(a, b):
    return lax.dot_general(a, b, (((1,), (1,)), ((), ())), preferred_element_type=F32)


def _rms(x, g):
    return x * lax.rsqrt(jnp.mean(x * x, axis=-1, keepdims=True) + RMS_EPS) * g


def _rope(x, cos, sin_signed, half):
    n = x.shape[-1]
    lane = lax.broadcasted_iota(jnp.int32, x.shape, 1)
    first = (lane % (2 * half)) < half
    rot = jnp.where(first, pltpu.roll(x, n - half, 1), pltpu.roll(x, half, 1))
    return x * cos + rot * sin_signed


def _log_sigmoid_pair(z):
    sp = jnp.log1p(jnp.exp(-jnp.abs(z)))
    return jnp.minimum(z, 0.0) - sp, jnp.minimum(-z, 0.0) - sp


def _head_mask(shape, h, dh):
    lane = lax.broadcasted_iota(jnp.int32, shape, len(shape) - 1)
    return (lane >= h * dh) & (lane < (h + 1) * dh)


def _suffix_sum(lk, upper):
    hi = lk.astype(BF)
    lo = (lk - hi.astype(F32)).astype(BF)
    return _dot(hi, upper) + _dot(lo, upper)


def _strict_upper(n):
    r = lax.broadcasted_iota(jnp.int32, (n, n), 0)
    c = lax.broadcasted_iota(jnp.int32, (n, n), 1)
    return jnp.where(r > c, 1.0, 0.0).astype(BF)


def _ffn_body(x_ref, g_ref, wgu_ref, wd_ref, gf_ref, o_ref, a_scr, *, d_ff, chunk, final):
    x = x_ref[...]
    h = _rms(x, g_ref[...]).astype(BF)
    for c in range(d_ff // chunk):
        gate = _dot(h, wgu_ref[:, c * chunk:(c + 1) * chunk])
        up = _dot(h, wgu_ref[:, d_ff + c * chunk:d_ff + (c + 1) * chunk])
        a_scr[:, c * chunk:(c + 1) * chunk] = (gate / (1.0 + jnp.exp(-gate)) * up).astype(BF)
    y = x + 0.5 * _dot(a_scr[...], wd_ref[...])
    if final:
        y = _rms(y, gf_ref[...])
    o_ref[...] = y


def _ffn(x, g, wgu, wd, gf, *, final):
    t, d = x.shape
    f = wd.shape[0]
    tm = min(512, t)
    return pl.pallas_call(
        functools.partial(_ffn_body, d_ff=f, chunk=256, final=final),
        grid=(t // tm,),
        in_specs=[pl.BlockSpec((tm, d), lambda i: (i, 0)), _resident((1, d)), _resident((d, 2 * f)),
                  _resident((f, d)), _resident((1, d))],
        out_specs=pl.BlockSpec((tm, d), lambda i: (i, 0)),
        out_shape=jax.ShapeDtypeStruct((t, d), F32),
        scratch_shapes=[pltpu.VMEM((tm, f), BF)],
        compiler_params=_cp("parallel"), name="ffn_half")(x, g, wgu, wd, gf)


def _norm_mm_body(x_ref, g_ref, w_ref, k_ref, v_ref):
    y = _dot(_rms(x_ref[...], g_ref[...]).astype(BF), w_ref[...])
    n = k_ref.shape[-1]
    k_ref[...] = y[:, :n]
    v_ref[...] = y[:, n:]


def _mem_kv(mem, g, w):
    t, d = mem.shape
    n = w.shape[1] // 2
    tm = min(512, t)
    return pl.pallas_call(
        _norm_mm_body, grid=(t // tm,),
        in_specs=[pl.BlockSpec((tm, d), lambda i: (i, 0)), _resident((1, d)), _resident((d, 2 * n))],
        out_specs=[pl.BlockSpec((tm, n), lambda i: (i, 0))] * 2,
        out_shape=[jax.ShapeDtypeStruct((t, n), F32)] * 2,
        compiler_params=_cp("parallel"), name="mem_kv")(mem, g, w)


def _even_proj_body(x_ref, g_ref, win_ref, gq_ref, gkv_ref, wuq_ref, wuk_ref, cos_ref, sin_ref,
                    qm_ref, km_ref, lat_ref, kpe_ref, qb_ref, kbb_ref, vbb_ref, kb_ref, vb_ref):
    h = _rms(x_ref[...], g_ref[...]).astype(BF)
    z = _dot(h, win_ref[...])
    cos, sin = cos_ref[...], sin_ref[...]
    c_q = _rms(z[:, :R_Q], gq_ref[...]).astype(BF)
    c_kv = _rms(z[:, R_Q:R_Q + R_KV], gkv_ref[...])
    kpe = _rope(z[:, 384:512], cos, sin, DH_ROPE // 2)
    lat_ref[...] = c_kv
    kpe_ref[...] = kpe[:, :DH_ROPE]
    km_ref[:, :R_KV] = c_kv.astype(BF)
    km_ref[:, R_KV:] = kpe.astype(BF)
    qb_ref[...] = z[:, 512:1024].astype(BF)
    kb = z[:, 1024:1536]
    vb = z[:, 1536:2048]
    kb_ref[...] = kb
    vb_ref[...] = vb
    kbb_ref[...] = kb.astype(BF)
    vbb_ref[...] = vb.astype(BF)
    q = _dot(c_q, wuq_ref[...])
    for hp in range(H_A // 2):
        ql = _dot(q[:, hp * 128:(hp + 1) * 128].astype(BF), wuk_ref[hp])
        for e in range(2):
            hh = 2 * hp + e
            qr = _rope(q[:, 512 + hh * 128:512 + (hh + 1) * 128], cos, sin, DH_ROPE // 2)
            qm_ref[hh, :, :R_KV] = ql[:, e * 128:(e + 1) * 128].astype(BF)
            qm_ref[hh, :, R_KV:] = qr.astype(BF)


def _even_project(x, g, win, gq, gkv, wuq, wuk, cos, sin):
    t, d = x.shape
    tm = min(512, t)
    nper = cos.shape[0] // tm
    row = lambda i: (i, 0)
    outs = [((H_A, t, 256), BF, pl.BlockSpec((H_A, tm, 256), lambda i: (0, i, 0))),
            ((t, 256), BF, pl.BlockSpec((tm, 256), row)),
            ((t, R_KV), F32, pl.BlockSpec((tm, R_KV), row)),
            ((t, DH_ROPE), F32, pl.BlockSpec((tm, DH_ROPE), row)),
            ((t, 512), BF, pl.BlockSpec((tm, 512), row)),
            ((t, 512), BF, pl.BlockSpec((tm, 512), row)),
            ((t, 512), BF, pl.BlockSpec((tm, 512), row)),
            ((t, 512), F32, pl.BlockSpec((tm, 512), row)),
            ((t, 512), F32, pl.BlockSpec((tm, 512), row))]
    return pl.pallas_call(
        _even_proj_body, grid=(t // tm,),
        in_specs=[pl.BlockSpec((tm, d), row), _resident((1, d)), _resident(win.shape), _resident((1, R_Q)),
                  _resident((1, R_KV)), _resident(wuq.shape), _resident(wuk.shape),
                  pl.BlockSpec((tm, LANES), lambda i: (i % nper, 0)),
                  pl.BlockSpec((tm, LANES), lambda i: (i % nper, 0))],
        out_specs=[o[2] for o in outs],
        out_shape=[jax.ShapeDtypeStruct(o[0], o[1]) for o in outs],
        compiler_params=_cp("parallel"), name="even_project")(x, g, win, gq, gkv, wuq, wuk, cos, sin)


def _mla_prompt_body(q_ref, kv_ref, o_ref, m_scr, l_scr, acc_scr):
    i = pl.program_id(1)
    rows = H_A * BLK
    q = q_ref[...].reshape(rows, 256)
    m_scr[...] = jnp.full((rows, LANES), NEG_INF, F32)
    l_scr[...] = jnp.zeros((rows, LANES), F32)
    acc_scr[...] = jnp.zeros((rows, R_KV), F32)
    tk = 2 * BLK
    c = MLA_SCALE * math.log2(math.e)

    def fold(off, causal):
        k = kv_ref[pl.ds(off, tk), :]
        s = _dot_nt(q, k)
        if causal:
            qpos = i * BLK + lax.broadcasted_iota(jnp.int32, (rows, tk), 0) % BLK
            s = jnp.where(off + lax.broadcasted_iota(jnp.int32, (rows, tk), 1) <= qpos, s, NEG_INF)
        m_prev = m_scr[...]
        m_new = jnp.maximum(m_prev, jnp.max(s, axis=1, keepdims=True))
        p = jnp.exp2((s - jnp.concatenate([m_new, m_new], axis=1)) * c)
        alpha = jnp.exp2((m_prev - m_new) * c)
        l_scr[...] = alpha * l_scr[...] + jnp.sum(p, axis=1, keepdims=True)
        acc_scr[...] = alpha * acc_scr[...] + _dot(p.astype(BF), k[:, :R_KV])
        m_scr[...] = m_new

    n_full = i // 2

    def full(j, carry):
        fold(pl.multiple_of(j * tk, tk), False)
        return carry

    lax.fori_loop(0, n_full, full, 0)
    fold(pl.multiple_of(n_full * tk, tk), True)
    o_ref[...] = (acc_scr[...] / l_scr[...]).reshape(H_A, BLK, R_KV).astype(BF)


def _mla_prompt(qm, km, b, s):
    nq = s // BLK
    rows = H_A * BLK
    return pl.pallas_call(
        _mla_prompt_body, grid=(b, nq),
        in_specs=[pl.BlockSpec((H_A, BLK, 256), lambda bb, i: (0, bb * nq + i, 0)),
                  pl.BlockSpec((s, 256), lambda bb, i: (bb, 0))],
        out_specs=pl.BlockSpec((H_A, BLK, R_KV), lambda bb, i: (0, bb * nq + i, 0)),
        out_shape=jax.ShapeDtypeStruct((H_A, b * s, R_KV), BF),
        scratch_shapes=[pltpu.VMEM((rows, LANES), F32), pltpu.VMEM((rows, LANES), F32),
                        pltpu.VMEM((rows, R_KV), F32)],
        compiler_params=_cp("parallel", "parallel"), name="mla_prompt")(qm, km)


def _sb_prompt_body(q_ref, k_ref, v_ref, o_ref, qe_scr, c_scr, acc_scr):
    i = pl.program_id(1)
    rows = H_B * BLK
    first = _head_mask((BLK, LANES), 0, DH_B)
    upper = _strict_upper(BLK)
    for p in range(H_B // 2):
        q = q_ref[:, p * LANES:(p + 1) * LANES].astype(F32)
        qe_scr[2 * p] = jnp.where(first, q, 0.0).astype(BF)
        qe_scr[2 * p + 1] = jnp.where(first, 0.0, q).astype(BF)
    c_scr[...] = jnp.zeros(c_scr.shape, F32)
    acc_scr[...] = jnp.zeros(acc_scr.shape, F32)

    def fold(j, diagonal):
        off = pl.multiple_of(j * BLK, BLK)
        z = jnp.concatenate(
            [_dot_nt(qe_scr[h], k_ref[pl.ds(off, BLK), (h // 2) * LANES:(h // 2 + 1) * LANES]) for h in range(H_B)],
            axis=0) * SB_SCALE
        ls_pos, ls_neg = _log_sigmoid_pair(z)
        if diagonal:
            valid = (lax.broadcasted_iota(jnp.int32, (rows, BLK), 1)
                     < lax.broadcasted_iota(jnp.int32, (rows, BLK), 0) % BLK)
            ls_neg = jnp.where(valid, ls_neg, 0.0)
        c = c_scr[...]
        w = jnp.exp(ls_pos + _suffix_sum(ls_neg, upper) + c)
        if diagonal:
            w = jnp.where(valid, w, 0.0)
        w = w.astype(BF)
        for p in range(H_B // 2):
            v = v_ref[pl.ds(off, BLK), p * LANES:(p + 1) * LANES]
            d0 = _dot(w[2 * p * BLK:(2 * p + 1) * BLK], v)
            d1 = _dot(w[(2 * p + 1) * BLK:(2 * p + 2) * BLK], v)
            acc_scr[:, p * LANES:(p + 1) * LANES] += jnp.where(first, d0, d1)
        c = c + jnp.sum(ls_neg, axis=1, keepdims=True)
        c_scr[...] = c
        return (jnp.max(c) > SB_DEAD).astype(jnp.int32)

    alive = fold(i, True)
    lax.while_loop(lambda st: (st[0] >= 0) & (st[1] > 0),
                   lambda st: (st[0] - 1, fold(st[0], False)), (i - 1, alive))
    o_ref[...] = acc_scr[...].astype(BF)


def _sb_prompt(qb, kb, vb, b, s):
    nq = s // BLK
    w = H_B * DH_B
    return pl.pallas_call(
        _sb_prompt_body, grid=(b, nq),
        in_specs=[pl.BlockSpec((BLK, w), lambda bb, i: (bb * nq + i, 0)),
                  pl.BlockSpec((s, w), lambda bb, i: (bb, 0)),
                  pl.BlockSpec((s, w), lambda bb, i: (bb, 0))],
        out_specs=pl.BlockSpec((BLK, w), lambda bb, i: (bb * nq + i, 0)),
        out_shape=jax.ShapeDtypeStruct((b * s, w), BF),
        scratch_shapes=[pltpu.VMEM((H_B, BLK, LANES), BF), pltpu.VMEM((H_B * BLK, 1), F32),
                        pltpu.VMEM((BLK, w), F32)],
        compiler_params=_cp("parallel", "parallel"), name="sb_prompt")(qb, kb, vb)


def _even_out_body(x_ref, ol_ref, ob_ref, wuv_ref, wout_ref, o_ref):
    parts = []
    for hp in range(H_A // 2):
        ol = jnp.concatenate([ol_ref[2 * hp], ol_ref[2 * hp + 1]], axis=1)
        parts.append(_dot(ol, wuv_ref[hp]).astype(BF))
    oa = jnp.concatenate(parts, axis=1)
    na = H_A * DH_V
    o_ref[...] = x_ref[...] + _dot(oa, wout_ref[:na, :]) + _dot(ob_ref[...], wout_ref[na:, :])


def _even_output(x, olat, ob, wuv, wout):
    t, d = x.shape
    tm = min(512, t)
    row = lambda i: (i, 0)
    return pl.pallas_call(
        _even_out_body, grid=(t // tm,),
        in_specs=[pl.BlockSpec((tm, d), row), pl.BlockSpec((H_A, tm, R_KV), lambda i: (0, i, 0)),
                  pl.BlockSpec((tm, 512), row), _resident(wuv.shape), _resident(wout.shape)],
        out_specs=pl.BlockSpec((tm, d), row),
        out_shape=jax.ShapeDtypeStruct((t, d), F32),
        compiler_params=_cp("parallel"), name="even_output")(x, olat, ob, wuv, wout)


def _mla_sample_body(pt_ref, q_ref, new_ref, *rest, group, t_new):
    lat_refs = rest[:group]
    kr_refs = rest[group:2 * group]
    o_ref, m_scr, l_scr, acc_scr, lat_scr, kr_scr = rest[2 * group:]
    p = pl.program_id(1)
    rows = q_ref.shape[1]

    @pl.when(p == 0)
    def _():
        m_scr[...] = jnp.full((rows, LANES), NEG_INF, F32)
        l_scr[...] = jnp.zeros((rows, LANES), F32)
        acc_scr[...] = jnp.zeros((rows, R_KV), F32)

    q = q_ref[0]
    ql = q[:, :R_KV]
    qr = q[:, R_KV:R_KV + DH_ROPE]
    for g in range(group):
        lat_scr[g * PAGE:(g + 1) * PAGE, :] = lat_refs[g][0].astype(BF)
        kr_scr[:, g * PAGE:(g + 1) * PAGE] = kr_refs[g][0].astype(BF)
    lat = lat_scr[...]
    s = (_dot_nt(ql, lat) + _dot(qr, kr_scr[...])) * MLA_SCALE
    m_prev = m_scr[...]
    m_new = jnp.maximum(m_prev, jnp.max(s, axis=1, keepdims=True))
    pr = jnp.exp(s - m_new[:, :1])
    alpha = jnp.exp(m_prev - m_new)
    l_scr[...] = alpha * l_scr[...] + jnp.sum(pr, axis=1, keepdims=True)
    acc_scr[...] = alpha * acc_scr[...] + _dot(pr.astype(BF), lat)
    m_scr[...] = m_new

    @pl.when(p == pl.num_programs(1) - 1)
    def _():
        qf = q.astype(F32)
        tq = lax.broadcasted_iota(jnp.int32, (rows, 1), 0) // H_A
        m = m_scr[...]
        l = l_scr[...]
        acc = acc_scr[...]
        for j in range(t_new):
            kn = new_ref[0, j:j + 1, :].astype(F32)
            s = jnp.sum(qf * kn, axis=1, keepdims=True) * MLA_SCALE
            s = jnp.where(j <= tq, s, NEG_INF)
            m_new = jnp.maximum(m, s)
            pj = jnp.exp(s - m_new)
            alpha = jnp.exp(m - m_new)
            l = alpha * l + pj
            acc = alpha * acc + pj.astype(BF).astype(F32) * kn[:, :R_KV]
            m = m_new
        o_ref[0] = (acc / l).astype(BF)


def _mla_sample(page_table, q, new, lat_pool, kr_pool, group):
    bd, n_pages = page_table.shape
    rows = q.shape[1]
    t_new = new.shape[1]
    lat_specs = [pl.BlockSpec((1, PAGE, R_KV), functools.partial(
        lambda b, p, pt, g: (pt[b, p * group + g], 0, 0), g=g)) for g in range(group)]
    kr_specs = [pl.BlockSpec((1, DH_ROPE, PAGE), functools.partial(
        lambda b, p, pt, g: (pt[b, p * group + g], 0, 0), g=g)) for g in range(group)]
    grid_spec = pltpu.PrefetchScalarGridSpec(
        num_scalar_prefetch=1, grid=(bd, n_pages // group),
        in_specs=[pl.BlockSpec((1, rows, 256), lambda b, p, pt: (b, 0, 0)),
                  pl.BlockSpec((1, t_new, 256), lambda b, p, pt: (b, 0, 0))] + lat_specs + kr_specs,
        out_specs=pl.BlockSpec((1, rows, R_KV), lambda b, p, pt: (b, 0, 0)),
        scratch_shapes=[pltpu.VMEM((rows, LANES), F32), pltpu.VMEM((rows, LANES), F32),
                        pltpu.VMEM((rows, R_KV), F32), pltpu.VMEM((group * PAGE, R_KV), BF),
                        pltpu.VMEM((DH_ROPE, group * PAGE), BF)])
    return pl.pallas_call(
        functools.partial(_mla_sample_body, group=group, t_new=t_new), grid_spec=grid_spec,
        out_shape=jax.ShapeDtypeStruct((bd, rows, R_KV), BF),
        compiler_params=_cp("parallel", "arbitrary"), name="mla_sample")(
            page_table, q, new, *([lat_pool] * group), *([kr_pool] * group))


def _sb_sample_body(pt_ref, q_ref, kn_ref, vn_ref, k_hbm, v_hbm, o_ref, kbuf, vbuf, sem, c_scr, acc_scr,
                    *, layer, n_pages, t_new):
    b = pl.program_id(0)
    rows = H_B * 8
    slot_t = lax.broadcasted_iota(jnp.int32, (rows, 1), 0) % 8
    real = slot_t < t_new

    def copies(p, slot):
        pg = pt_ref[b, p]
        return (pltpu.make_async_copy(k_hbm.at[layer, pg], kbuf.at[slot], sem.at[0, slot]),
                pltpu.make_async_copy(v_hbm.at[layer, pg], vbuf.at[slot], sem.at[1, slot]))

    def start(p, slot):
        for cp in copies(p, slot):
            cp.start()

    def wait(p, slot):
        for cp in copies(p, slot):
            cp.wait()

    start(n_pages - 1, 0)

    q = q_ref[0]
    qf = q.astype(BF).astype(F32)
    q3 = q.reshape(H_B, 8, DH_B).astype(BF)
    c = jnp.zeros((rows, 1), F32)
    acc = jnp.zeros((rows, DH_B), F32)
    for j in reversed(range(t_new)):
        kn = kn_ref[0, j].astype(BF).astype(F32)
        vn = vn_ref[0, j].astype(BF).astype(F32)
        z = jnp.sum(qf * kn, axis=1, keepdims=True) * SB_SCALE
        ls_pos, ls_neg = _log_sigmoid_pair(z)
        valid = j < slot_t
        w = jnp.where(valid, jnp.exp(ls_pos + c), 0.0)
        acc = acc + w.astype(BF).astype(F32) * vn
        c = c + jnp.where(valid, ls_neg, 0.0)
    c_scr[...] = c
    acc_scr[...] = acc
    upper = _strict_upper(PAGE)

    def alive_of(cc):
        return (jnp.max(jnp.where(real, cc, NEG_INF)) > SB_DEAD).astype(jnp.int32)

    def body(st):
        p, _ = st
        slot = lax.rem(n_pages - 1 - p, 2)
        wait(p, slot)

        @pl.when(p >= 1)
        def _():
            start(p - 1, 1 - slot)

        kt = kbuf[slot].astype(BF)
        vt = vbuf[slot].astype(BF)
        z = lax.dot_general(q3, kt, (((2,), (1,)), ((0,), (0,))), preferred_element_type=F32)
        z = z.reshape(rows, PAGE) * SB_SCALE
        ls_pos, ls_neg = _log_sigmoid_pair(z)
        cc = c_scr[...]
        w = jnp.exp(ls_pos + _suffix_sum(ls_neg, upper) + cc).reshape(H_B, 8, PAGE).astype(BF)
        d = lax.dot_general(w, vt, (((2,), (2,)), ((0,), (0,))), preferred_element_type=F32)
        acc_scr[...] = acc_scr[...] + d.reshape(rows, DH_B)
        cc = cc + jnp.sum(ls_neg, axis=1, keepdims=True)
        c_scr[...] = cc
        return p - 1, alive_of(cc)

    p_next, _ = lax.while_loop(lambda st: (st[0] >= 0) & (st[1] > 0), body, (n_pages - 1, alive_of(c)))

    @pl.when(p_next >= 0)
    def _():
        wait(p_next, lax.rem(n_pages - 1 - p_next, 2))

    o_ref[0] = acc_scr[...]


def _sb_sample(page_table, q, k_new, v_new, k_pool, v_pool, layer):
    bd, n_pages = page_table.shape
    t_new = q.shape[1]
    rows = H_B * 8
    q_rows = jnp.pad(jnp.transpose(q, (0, 2, 1, 3)), ((0, 0), (0, 0), (0, 8 - t_new), (0, 0))).reshape(bd, rows, DH_B)
    spread = lambda a: jnp.broadcast_to(a[:, :, :, None, :], (bd, t_new, H_B, 8, DH_B)).reshape(bd, t_new, rows, DH_B)
    own3 = lambda b, pt: (b, 0, 0)
    own4 = lambda b, pt: (b, 0, 0, 0)
    grid_spec = pltpu.PrefetchScalarGridSpec(
        num_scalar_prefetch=1, grid=(bd,),
        in_specs=[pl.BlockSpec((1, rows, DH_B), own3), pl.BlockSpec((1, t_new, rows, DH_B), own4),
                  pl.BlockSpec((1, t_new, rows, DH_B), own4),
                  pl.BlockSpec(memory_space=pl.ANY), pl.BlockSpec(memory_space=pl.ANY)],
        out_specs=pl.BlockSpec((1, rows, DH_B), own3),
        scratch_shapes=[pltpu.VMEM((2, H_B, DH_B, PAGE), F32), pltpu.VMEM((2, H_B, DH_B, PAGE), F32),
                        pltpu.SemaphoreType.DMA((2, 2)), pltpu.VMEM((rows, 1), F32), pltpu.VMEM((rows, DH_B), F32)])
    by_head = lambda pool: jnp.transpose(pool, (0, 1, 3, 4, 2))
    out = pl.pallas_call(
        functools.partial(_sb_sample_body, layer=layer, n_pages=n_pages, t_new=t_new), grid_spec=grid_spec,
        out_shape=jax.ShapeDtypeStruct((bd, rows, DH_B), F32),
        compiler_params=_cp("arbitrary"), name="sb_sample")(
            page_table, q_rows, spread(k_new), spread(v_new), by_head(k_pool), by_head(v_pool))
    return jnp.transpose(out.reshape(bd, H_B, 8, DH_B)[:, :, :t_new], (0, 2, 1, 3)).reshape(bd * t_new, H_B * DH_B)


def _odd_proj_body(x_ref, g_ref, win_ref, cos_ref, sin_ref, z_ref):
    h = _rms(x_ref[...], g_ref[...]).astype(BF)
    z = _dot(h, win_ref[...])
    cos, sin = cos_ref[...], sin_ref[...]
    w = H_C * DH_C
    for c in range(z.shape[1] // w):
        part = z[:, c * w:(c + 1) * w]
        if c % 3 != 2:
            part = _rope(part, cos, sin, DH_C // 2)
        z_ref[:, c * w:(c + 1) * w] = part


def _odd_project(x, g, win, cos, sin):
    t, d = x.shape
    n = win.shape[1]
    tm = min(512, t)
    nper = cos.shape[0] // tm
    w = H_C * DH_C
    return pl.pallas_call(
        _odd_proj_body, grid=(t // tm,),
        in_specs=[pl.BlockSpec((tm, d), lambda i: (i, 0)), _resident((1, d)), _resident(win.shape),
                  pl.BlockSpec((tm, w), lambda i: (i % nper, 0)), pl.BlockSpec((tm, w), lambda i: (i % nper, 0))],
        out_specs=pl.BlockSpec((tm, n), lambda i: (i, 0)),
        out_shape=jax.ShapeDtypeStruct((t, n), F32),
        compiler_params=_cp("parallel"), name="odd_project")(x, g, win, cos, sin)


def _dil_prompt_body(q0_ref, q1_ref, k0_ref, k1_ref, v0_ref, v1_ref, o_ref, lse_ref, *, dil, nback):
    s_len = q0_ref.shape[0]
    w = H_C * DH_C
    win = BLK * dil
    both = lambda r0, r1, rows: jnp.concatenate([r0[rows, :], r1[rows, :]], axis=1)
    row = lax.broadcasted_iota(jnp.int32, (H_C * BLK, 2 * BLK), 0) % BLK
    col = lax.broadcasted_iota(jnp.int32, (H_C * BLK, 2 * BLK), 1)
    diff = row + BLK - col
    band = (diff >= 0) & (diff <= nback)
    masks = [_head_mask((BLK, w), h, DH_C) for h in range(H_C)]

    def block(jb, carry):
        base = pl.multiple_of(jb * win, win)
        prev = pl.multiple_of(jnp.maximum(jb - 1, 0) * win, win)
        first_col = jnp.where(jb > 0, 0, BLK)
        mask = band & (col >= first_col)
        for r in range(dil):
            cur = pl.ds(base + r, BLK, stride=dil)
            old = pl.ds(prev + r, BLK, stride=dil)
            q = both(q0_ref, q1_ref, cur)
            kk = jnp.concatenate([both(k0_ref, k1_ref, old), both(k0_ref, k1_ref, cur)], axis=0).astype(BF)
            vv = jnp.concatenate([both(v0_ref, v1_ref, old), both(v0_ref, v1_ref, cur)], axis=0).astype(BF)
            s = jnp.concatenate([_dot_nt(jnp.where(masks[h], q, 0.0).astype(BF), kk) for h in range(H_C)], axis=0)
            s = jnp.where(mask, s * DIL_SCALE, NEG_INF)
            m = jnp.max(s, axis=1, keepdims=True)
            p = jnp.exp(s - m)
            l = jnp.sum(p, axis=1, keepdims=True)
            pb = p.astype(BF)
            lse_rows = m + jnp.log(l)
            o = jnp.zeros((BLK, w), F32)
            lse = jnp.zeros((BLK, w), F32)
            for h in range(H_C):
                rows_h = slice(h * BLK, (h + 1) * BLK)
                o = jnp.where(masks[h], _dot(pb[rows_h], vv) / l[rows_h], o)
                lse = jnp.where(masks[h], lse_rows[rows_h], lse)
            for half in range(2):
                o_ref[half, cur, :] = o[:, half * LANES:(half + 1) * LANES]
                lse_ref[half, cur, :] = lse[:, half * LANES:(half + 1) * LANES]
        return carry

    lax.fori_loop(0, s_len // win, block, 0)


def _dil_prompt(zr, b, s, g, dil, nback):
    spec = lambda part, half: pl.BlockSpec((s, LANES), lambda bb: (bb, 2 * (3 * g + part) + half))
    out_spec = pl.BlockSpec((2, s, LANES), lambda bb: (0, bb, 0))
    return pl.pallas_call(
        functools.partial(_dil_prompt_body, dil=dil, nback=nback), grid=(b,),
        in_specs=[spec(part, half) for part in range(3) for half in range(2)],
        out_specs=[out_spec, out_spec],
        out_shape=[jax.ShapeDtypeStruct((2, b * s, LANES), F32)] * 2,
        compiler_params=_cp("parallel"), name=f"dil_prompt_{g}")(*([zr] * 6))


def _dil_sample_body(q_ref, z_ref, b0_ref, b1_ref, b2_ref, n0_ref, n1_ref, n2_ref,
                     o_ref, lse_ref, u0_ref, u1_ref, u2_ref, *, t_new):
    w = H_C * DH_C
    rows = H_C * 8
    bufs = (b0_ref, b1_ref, b2_ref)
    news = (n0_ref, n1_ref, n2_ref)
    upds = (u0_ref, u1_ref, u2_ref)
    slot = lax.broadcasted_iota(jnp.int32, (rows, 1), 0) % 8
    pick = (lax.broadcasted_iota(jnp.int32, (rows, w), 0) // 8
            == lax.broadcasted_iota(jnp.int32, (rows, w), 1) // DH_C)
    fold = lambda a: functools.reduce(lambda x, y: x + y, [a[h * 8:(h + 1) * 8] for h in range(H_C)])
    for g, (window, dil) in enumerate(DIL_GROUPS):
        q = q_ref[0, g * rows:(g + 1) * rows, :].astype(BF)
        qf = q.astype(F32)
        knew = z_ref[0, :, (3 * g + 1) * w:(3 * g + 2) * w].astype(BF).astype(F32)
        vnew = z_ref[0, :, (3 * g + 2) * w:(3 * g + 3) * w].astype(BF).astype(F32)
        kt = bufs[g][0, 0].reshape(w, window).astype(BF)
        vt = bufs[g][0, 1].reshape(w, window).astype(BF)
        col = lax.broadcasted_iota(jnp.int32, (rows, window), 1)
        valid = (col >= slot) & (col % dil == slot % dil)
        s_buf = jnp.where(valid, _dot(q, kt) * DIL_SCALE, NEG_INF)
        s_new = []
        for j in range(t_new):
            sj = jnp.sum(qf * knew[j:j + 1, :], axis=1, keepdims=True) * DIL_SCALE
            s_new.append(jnp.where((j <= slot) & ((slot - j) % dil == 0), sj, NEG_INF))
        m = functools.reduce(jnp.maximum, s_new, jnp.max(s_buf, axis=1, keepdims=True))
        p_buf = jnp.exp(s_buf - m)
        l = jnp.sum(p_buf, axis=1, keepdims=True)
        o = _dot_nt(p_buf.astype(BF), vt)
        for j, sj in enumerate(s_new):
            pj = jnp.exp(sj - m)
            l = l + pj
            o = o + pj.astype(BF).astype(F32) * vnew[j:j + 1, :]
        o = jnp.where(pick, o / l, 0.0)
        lse = jnp.where(pick, m + jnp.log(l), 0.0)
        o_ref[0, :, g * w:(g + 1) * w] = fold(o)[:t_new]
        lse_ref[0, :, g * w:(g + 1) * w] = fold(lse)[:t_new]
        x = bufs[g][0].reshape(2 * w, window)
        moved = pltpu.roll(x, window - t_new, 1)
        tail = jnp.where(lax.broadcasted_iota(jnp.int32, (2 * w, LANES), 1) >= LANES - t_new,
                         news[g][0], moved[:, window - LANES:])
        if window > LANES:
            tail = jnp.concatenate([moved[:, :window - LANES], tail], axis=1)
        upds[g][0] = tail.reshape(2, H_C, DH_C, window)


def _dil_sample(q_rows, zr, bufs):
    bd, t_new, n = zr.shape
    w = H_C * DH_C
    own = lambda b: (b, 0, 0)
    views = [jnp.transpose(buf, (0, 2, 3, 4, 1)) for buf in bufs]
    buf_specs = [pl.BlockSpec((1,) + v.shape[1:], lambda b: (b, 0, 0, 0, 0)) for v in views]
    ng = len(DIL_GROUPS)
    news = [jnp.pad(jnp.swapaxes(zr[:, :, (3 * g + 1) * w:(3 * g + 3) * w], 1, 2),
                    ((0, 0), (0, 0), (LANES - t_new, 0))) for g in range(ng)]
    new_spec = pl.BlockSpec((1, 2 * w, LANES), own)
    outs = pl.pallas_call(
        functools.partial(_dil_sample_body, t_new=t_new), grid=(bd,),
        in_specs=[pl.BlockSpec((1, q_rows.shape[1], w), own), pl.BlockSpec((1, t_new, n), own)] + buf_specs
                 + [new_spec] * ng,
        out_specs=[pl.BlockSpec((1, t_new, ng * w), own)] * 2 + buf_specs,
        out_shape=[jax.ShapeDtypeStruct((bd, t_new, ng * w), F32)] * 2
                  + [jax.ShapeDtypeStruct(v.shape, F32) for v in views],
        compiler_params=_cp("parallel"), name="dil_sample")(q_rows, zr, *views, *news)
    return outs[0], outs[1], [jnp.transpose(u, (0, 4, 1, 2, 3)) for u in outs[2:]]


def _odd_out_body(x_ref, *rest):
    ng = len(DIL_GROUPS)
    o_refs, l_refs = rest[:ng], rest[ng:2 * ng]
    w_ref, out_ref = rest[2 * ng:]
    whole = lambda r: jnp.concatenate([r[0], r[1]], axis=1)
    ls = [whole(r) for r in l_refs]
    m = functools.reduce(jnp.maximum, ls)
    es = [jnp.exp(l - m) for l in ls]
    den = functools.reduce(lambda a, b: a + b, es)
    mix = functools.reduce(lambda a, b: a + b, [e / den * whole(r) for e, r in zip(es, o_refs)])
    out_ref[...] = x_ref[...] + _dot(mix.astype(BF), w_ref[...])


def _odd_output(x, outs, lses, wout):
    t, d = x.shape
    tm = min(512, t)
    row = lambda i: (i, 0)
    halves = pl.BlockSpec((2, tm, LANES), lambda i: (0, i, 0))
    return pl.pallas_call(
        _odd_out_body, grid=(t // tm,),
        in_specs=[pl.BlockSpec((tm, d), row)] + [halves] * (2 * len(outs)) + [_resident(wout.shape)],
        out_specs=pl.BlockSpec((tm, d), row),
        out_shape=jax.ShapeDtypeStruct((t, d), F32),
        compiler_params=_cp("parallel"), name="odd_output")(x, *outs, *lses, wout)


def _mem_attn_body(x_ref, g_ref, wq_ref, k_ref, v_ref, wo_ref, o_ref):
    x = x_ref[0]
    w = H_M * DH_M
    q = _dot(_rms(x, g_ref[...]).astype(BF), wq_ref[...])
    kk = k_ref[0].astype(BF)
    vv = v_ref[0].astype(BF)
    o = jnp.zeros((x.shape[0], w), F32)
    for h in range(H_M):
        hm = _head_mask(q.shape, h, DH_M)
        s = _dot_nt(jnp.where(hm, q, 0.0).astype(BF), kk) * MEM_SCALE
        m = jnp.max(s, axis=1, keepdims=True)
        p = jnp.exp(s - m)
        l = jnp.sum(p, axis=1, keepdims=True)
        o = jnp.where(hm, _dot(p.astype(BF), vv) / l, o)
    o_ref[0] = x + _dot(o.astype(BF), wo_ref[...])


def _mem_attend(x, g, wq, k, v, wo):
    b, s, d = x.shape
    n_mem, w = k.shape[1:]
    tm = min(512, s)
    return pl.pallas_call(
        _mem_attn_body, grid=(b, s // tm),
        in_specs=[pl.BlockSpec((1, tm, d), lambda bb, i: (bb, i, 0)), _resident((1, d)), _resident(wq.shape),
                  pl.BlockSpec((1, n_mem, w), lambda bb, i: (bb, 0, 0)),
                  pl.BlockSpec((1, n_mem, w), lambda bb, i: (bb, 0, 0)), _resident(wo.shape)],
        out_specs=pl.BlockSpec((1, tm, d), lambda bb, i: (bb, i, 0)),
        out_shape=jax.ShapeDtypeStruct((b, s, d), F32),
        compiler_params=_cp("parallel", "parallel"), name="mem_attend")(x, g, wq, k, v, wo)


def _rope_tables(pos, half, width):
    inv_freq = ROPE_THETA ** (-jnp.arange(half, dtype=F32) / half)
    ang = pos.astype(F32)[:, None] * inv_freq[None, :]
    cos, sin = jnp.cos(ang), jnp.sin(ang)
    reps = width // (2 * half)
    return (jnp.tile(jnp.concatenate([cos, cos], axis=1), (1, reps)),
            jnp.tile(jnp.concatenate([-sin, sin], axis=1), (1, reps)))


def _even_weights(w_in, w_uq, w_uk, w_uv):
    d = w_in.shape[0]
    c0 = R_Q + R_KV
    win = jnp.concatenate([w_in[:, :c0], w_in[:, c0:c0 + DH_ROPE], jnp.zeros((d, LANES - DH_ROPE), F32),
                           w_in[:, c0 + DH_ROPE:]], axis=1).astype(BF)
    nope = w_uq[:, :, :DH_NOPE].reshape(R_Q, H_A * DH_NOPE)
    rope = jnp.pad(w_uq[:, :, DH_NOPE:], ((0, 0), (0, 0), (0, LANES - DH_ROPE))).reshape(R_Q, H_A * LANES)
    wuq = jnp.concatenate([nope, rope], axis=1).astype(BF)
    uk = jnp.transpose(w_uk, (1, 2, 0)).reshape(H_A // 2, 2, DH_NOPE, R_KV)
    z = jnp.zeros_like(uk[:, 0])
    wuk = jnp.concatenate([jnp.concatenate([uk[:, 0], z], axis=2),
                           jnp.concatenate([z, uk[:, 1]], axis=2)], axis=1).astype(BF)
    uv = jnp.transpose(w_uv, (1, 0, 2)).reshape(H_A // 2, 2, R_KV, DH_V)
    zv = jnp.zeros_like(uv[:, 0])
    wuv = jnp.concatenate([jnp.concatenate([uv[:, 0], zv], axis=2),
                           jnp.concatenate([zv, uv[:, 1]], axis=2)], axis=1).astype(BF)
    return win, wuq, wuk, wuv


def kernel(x_prompt, x_sample, mem_prompt, cache_mla_latent, cache_mla_krope, cache_sb_k, cache_sb_v,
           cache_dil_w128, cache_dil_w512, cache_dil_w2048, cache_mem_k, cache_mem_v, page_table,
           ffn_norm, ffn_w_gu, ffn_w_down, mix_norm,
           ev_w_in, ev_g_q, ev_g_kv, ev_w_uq, ev_w_uk, ev_w_uv, ev_w_out,
           od_w_in, od_w_out,
           xa_norm, xa_mem_norm, xa_w_q, xa_w_kv, xa_w_o, final_norm):
    b, s, d = x_prompt.shape
    bd, t_new, _ = x_sample.shape
    n_mem = mem_prompt.shape[1]
    depth = mix_norm.shape[0]
    n_pages = page_table.shape[1]
    past = n_pages * PAGE
    pos_p = jnp.arange(s, dtype=jnp.int32)
    pos_s = jnp.tile(past + jnp.arange(t_new, dtype=jnp.int32), bd)
    dil_bufs = (cache_dil_w128, cache_dil_w512, cache_dil_w2048)
    w_hc = H_C * DH_C

    xp = x_prompt.reshape(b * s, d)
    xs = x_sample.reshape(bd * t_new, d)
    mem = mem_prompt.reshape(b * n_mem, d)
    row = lambda v: v.reshape(1, -1)

    p_lat, p_kpe, p_sbk, p_sbv, s_lat, s_kpe, s_sbk, s_sbv = ([] for _ in range(8))
    p_dil = [[] for _ in DIL_GROUPS]
    s_dil = [[] for _ in DIL_GROUPS]
    p_memk, p_memv = [], []

    for layer in range(depth):
        i = layer // 2
        wgu, wd = ffn_w_gu[layer, 0].astype(BF), ffn_w_down[layer, 0].astype(BF)
        xp = _ffn(xp, row(ffn_norm[layer, 0]), wgu, wd, row(final_norm), final=False)
        xs = _ffn(xs, row(ffn_norm[layer, 0]), wgu, wd, row(final_norm), final=False)
        if layer % 2 == 0:
            win, wuq, wuk, wuv = _even_weights(ev_w_in[i], ev_w_uq[i], ev_w_uk[i], ev_w_uv[i])
            wout = ev_w_out[i].astype(BF)
            args = (row(mix_norm[layer]), win, row(ev_g_q[i]), row(ev_g_kv[i]), wuq, wuk)
            cos, sin = _rope_tables(pos_p, DH_ROPE // 2, LANES)
            qm, km, lat, kpe, qb, kbb, vbb, kb, vb = _even_project(xp, *args, cos, sin)
            olat = _mla_prompt(qm, km, b, s)
            ob = _sb_prompt(qb, kbb, vbb, b, s)
            xp = _even_output(xp, olat, ob, wuv, wout)
            p_lat.append(lat.reshape(b, s, R_KV))
            p_kpe.append(kpe.reshape(b, s, DH_ROPE))
            p_sbk.append(kb.reshape(b, s, H_B, DH_B))
            p_sbv.append(vb.reshape(b, s, H_B, DH_B))
            cos, sin = _rope_tables(pos_s, DH_ROPE // 2, LANES)
            qm, km, lat, kpe, qb, kbb, vbb, kb, vb = _even_project(xs, *args, cos, sin)
            q_rows = jnp.transpose(qm.reshape(H_A, bd, t_new, 256), (1, 2, 0, 3)).reshape(bd, t_new * H_A, 256)
            olat = _mla_sample(page_table, q_rows, km.astype(F32).reshape(bd, t_new, 256),
                               cache_mla_latent[i], jnp.swapaxes(cache_mla_krope[i], 1, 2),
                               group=min(16, n_pages))
            olat = jnp.transpose(olat.reshape(bd * t_new, H_A, R_KV), (1, 0, 2))
            heads = lambda a: a.astype(F32).reshape(bd, t_new, H_B, DH_B)
            ob = _sb_sample(page_table, heads(qb), heads(kb), heads(vb), cache_sb_k, cache_sb_v, layer=i)
            xs = _even_output(xs, olat, ob.astype(BF), wuv, wout)
            s_lat.append(lat.reshape(bd, t_new, R_KV))
            s_kpe.append(kpe.reshape(bd, t_new, DH_ROPE))
            s_sbk.append(kb.reshape(bd, t_new, H_B, DH_B))
            s_sbv.append(vb.reshape(bd, t_new, H_B, DH_B))
        else:
            win = od_w_in[i].astype(BF)
            wout = od_w_out[i].astype(BF)
            cos, sin = _rope_tables(pos_p, DH_C // 2, w_hc)
            zp = _odd_project(xp, row(mix_norm[layer]), win, cos, sin)
            cos, sin = _rope_tables(pos_s, DH_C // 2, w_hc)
            zs = _odd_project(xs, row(mix_norm[layer]), win, cos, sin)
            outs, lses = [], []
            zp3 = zp.reshape(b, s, -1)
            zs3 = zs.reshape(bd, t_new, -1)
            bufs = []
            for g, (window, dil) in enumerate(DIL_GROUPS):
                o, lse = _dil_prompt(zp, b, s, g, dil, window // dil)
                outs.append(o)
                lses.append(lse)
                kv = zp3[:, s - min(window, s):, (3 * g + 1) * w_hc:(3 * g + 3) * w_hc]
                p_dil[g].append(kv.reshape(b, -1, 2, H_C, DH_C))
                buf = dil_bufs[g][i]
                assert buf.shape[1] == window and t_new <= 8
                bufs.append(buf)
            xp = _odd_output(xp, outs, lses, wout)
            ng = len(DIL_GROUPS)
            q_all = jnp.stack([zs3[:, :, 3 * g * w_hc:(3 * g + 1) * w_hc] for g in range(ng)], axis=1)
            q_all = jnp.pad(q_all, ((0, 0), (0, 0), (0, 8 - t_new), (0, 0)))
            q_rows = jnp.where(
                (jnp.arange(w_hc) // DH_C)[None, None, :] == jnp.arange(H_C)[:, None, None],
                q_all[:, :, None, :, :], 0.0).reshape(bd, ng * H_C * 8, w_hc)
            o_s, lse_s, new_bufs = _dil_sample(q_rows, zs3, bufs)
            for g in range(ng):
                s_dil[g].append(new_bufs[g])
            halves = lambda a: jnp.transpose(a.reshape(bd * t_new, ng, 2, LANES), (1, 2, 0, 3))
            o_s, lse_s = halves(o_s), halves(lse_s)
            xs = _odd_output(xs, [o_s[g] for g in range(ng)], [lse_s[g] for g in range(ng)], wout)
        mk, mv = _mem_kv(mem, row(xa_mem_norm[layer]), xa_w_kv[layer].astype(BF))
        p_memk.append(mk.reshape(b, n_mem, H_M, DH_M))
        p_memv.append(mv.reshape(b, n_mem, H_M, DH_M))
        wq, wo = xa_w_q[layer].astype(BF), xa_w_o[layer].astype(BF)
        xp = _mem_attend(xp.reshape(b, s, d), row(xa_norm[layer]), wq, mk.reshape(b, n_mem, -1),
                         mv.reshape(b, n_mem, -1), wo).reshape(b * s, d)
        xs = _mem_attend(xs.reshape(bd, t_new, d), row(xa_norm[layer]), wq,
                         cache_mem_k[layer].reshape(bd, n_mem, -1), cache_mem_v[layer].reshape(bd, n_mem, -1),
                         wo).reshape(bd * t_new, d)
        wgu, wd = ffn_w_gu[layer, 1].astype(BF), ffn_w_down[layer, 1].astype(BF)
        last = layer == depth - 1
        xp = _ffn(xp, row(ffn_norm[layer, 1]), wgu, wd, row(final_norm), final=last)
        xs = _ffn(xs, row(ffn_norm[layer, 1]), wgu, wd, row(final_norm), final=last)

    return (xp.reshape(b, s, d), xs.reshape(bd, t_new, d),
            jnp.stack(p_lat), jnp.stack(p_kpe), jnp.stack(p_sbk), jnp.stack(p_sbv),
            jnp.stack(p_dil[0]), jnp.stack(p_dil[1]), jnp.stack(p_dil[2]),
            jnp.stack(p_memk), jnp.stack(p_memv),
            jnp.stack(s_lat), jnp.stack(s_kpe), jnp.stack(s_sbk), jnp.stack(s_sbv),
            jnp.stack(s_dil[0]), jnp.stack(s_dil[1]), jnp.stack(s_dil[2]))
```

```python
import functools
import math

import jax
import jax.numpy as jnp
from jax import lax
from jax.experimental import pallas as pl
from jax.experimental.pallas import tpu as pltpu

F32 = jnp.float32
BF = jnp.bfloat16

H_A, DH_NOPE, DH_ROPE, DH_V, R_Q, R_KV = 8, 64, 32, 64, 256, 128
H_B, DH_B = 8, 64
DIL_GROUPS = ((128, 1), (512, 4), (2048, 16))
H_C, DH_C = 4, 64
H_M, DH_M = 4, 64
PAGE = 128
ROPE_THETA = 10000.0
RMS_EPS = 1e-6
NEG_INF = -1e30
MLA_SCALE = 1.0 / math.sqrt(DH_NOPE + DH_ROPE)
SB_SCALE = 1.0 / math.sqrt(DH_B)
DIL_SCALE = 1.0 / math.sqrt(DH_C)
MEM_SCALE = 1.0 / math.sqrt(DH_M)

LANES = 128
BLK = 128
SB_DEAD = -104.0
VMEM_LIMIT = 56 * 2**20


def _cp(*sem):
    return pltpu.CompilerParams(dimension_semantics=sem, vmem_limit_bytes=VMEM_LIMIT)


def _resident(shape):
    nd = len(shape)
    return pl.BlockSpec(shape, lambda *_: (0,) * nd, pipeline_mode=pl.Buffered(1))


def _dot(a, b):
    return jnp.dot(a, b, preferred_element_type=F32)


def _dot_nt(a, b):
    return lax.dot_general(a, b, (((1,), (1,)), ((), ())), preferred_element_type=F32)


def _rms(x, g):
    return x * lax.rsqrt(jnp.mean(x * x, axis=-1, keepdims=True) + RMS_EPS) * g


def _rope(x, cos, sin_signed, half):
    n = x.shape[-1]
    lane = lax.broadcasted_iota(jnp.int32, x.shape, 1)
    first = (lane % (2 * half)) < half
    rot = jnp.where(first, pltpu.roll(x, n - half, 1), pltpu.roll(x, half, 1))
    return x * cos + rot * sin_signed


def _log_sigmoid_pair(z):
    sp = jnp.log1p(jnp.exp(-jnp.abs(z)))
    return jnp.minimum(z, 0.0) - sp, jnp.minimum(-z, 0.0) - sp


def _head_mask(shape, h, dh):
    lane = lax.broadcasted_iota(jnp.int32, shape, len(shape) - 1)
    return (lane >= h * dh) & (lane < (h + 1) * dh)


def _suffix_sum(lk, upper):
    hi = lk.astype(BF)
    lo = (lk - hi.astype(F32)).astype(BF)
    return _dot(hi, upper) + _dot(lo, upper)


def _strict_upper(n):
    r = lax.broadcasted_iota(jnp.int32, (n, n), 0)
    c = lax.broadcasted_iota(jnp.int32, (n, n), 1)
    return jnp.where(r > c, 1.0, 0.0).astype(BF)


def _ffn_body(x_ref, g_ref, wgu_ref, wd_ref, gf_ref, o_ref, a_scr, *, d_ff, chunk, final):
    x = x_ref[...]
    h = _rms(x, g_ref[...]).astype(BF)
    for c in range(d_ff // chunk):
        gate = _dot(h, wgu_ref[:, c * chunk:(c + 1) * chunk])
        up = _dot(h, wgu_ref[:, d_ff + c * chunk:d_ff + (c + 1) * chunk])
        a_scr[:, c * chunk:(c + 1) * chunk] = (gate / (1.0 + jnp.exp(-gate)) * up).astype(BF)
    y = x + 0.5 * _dot(a_scr[...], wd_ref[...])
    if final:
        y = _rms(y, gf_ref[...])
    o_ref[...] = y


def _ffn(x, g, wgu, wd, gf, *, final):
    t, d = x.shape
    f = wd.shape[0]
    tm = min(512, t)
    return pl.pallas_call(
        functools.partial(_ffn_body, d_ff=f, chunk=256, final=final),
        grid=(t // tm,),
        in_specs=[pl.BlockSpec((tm, d), lambda i: (i, 0)), _resident((1, d)), _resident((d, 2 * f)),
                  _resident((f, d)), _resident((1, d))],
        out_specs=pl.BlockSpec((tm, d), lambda i: (i, 0)),
        out_shape=jax.ShapeDtypeStruct((t, d), F32),
        scratch_shapes=[pltpu.VMEM((tm, f), BF)],
        compiler_params=_cp("parallel"), name="ffn_half")(x, g, wgu, wd, gf)


def _norm_mm_body(x_ref, g_ref, w_ref, k_ref, v_ref):
    y = _dot(_rms(x_ref[...], g_ref[...]).astype(BF), w_ref[...])
    n = k_ref.shape[-1]
    k_ref[...] = y[:, :n]
    v_ref[...] = y[:, n:]


def _mem_kv(mem, g, w):
    t, d = mem.shape
    n = w.shape[1] // 2
    tm = min(512, t)
    return pl.pallas_call(
        _norm_mm_body, grid=(t // tm,),
        in_specs=[pl.BlockSpec((tm, d), lambda i: (i, 0)), _resident((1, d)), _resident((d, 2 * n))],
        out_specs=[pl.BlockSpec((tm, n), lambda i: (i, 0))] * 2,
        out_shape=[jax.ShapeDtypeStruct((t, n), F32)] * 2,
        compiler_params=_cp("parallel"), name="mem_kv")(mem, g, w)


def _even_proj_body(x_ref, g_ref, win_ref, gq_ref, gkv_ref, wuq_ref, wuk_ref, cos_ref, sin_ref,
                    qm_ref, km_ref, lat_ref, kpe_ref, qb_ref, kbb_ref, vbb_ref, kb_ref, vb_ref):
    h = _rms(x_ref[...], g_ref[...]).astype(BF)
    z = _dot(h, win_ref[...])
    cos, sin = cos_ref[...], sin_ref[...]
    c_q = _rms(z[:, :R_Q], gq_ref[...]).astype(BF)
    c_kv = _rms(z[:, R_Q:R_Q + R_KV], gkv_ref[...])
    kpe = _rope(z[:, 384:512], cos, sin, DH_ROPE // 2)
    lat_ref[...] = c_kv
    kpe_ref[...] = kpe[:, :DH_ROPE]
    km_ref[:, :R_KV] = c_kv.astype(BF)
    km_ref[:, R_KV:] = kpe.astype(BF)
    qb_ref[...] = z[:, 512:1024].astype(BF)
    kb = z[:, 1024:1536]
    vb = z[:, 1536:2048]
    kb_ref[...] = kb
    vb_ref[...] = vb
    kbb_ref[...] = kb.astype(BF)
    vbb_ref[...] = vb.astype(BF)
    q = _dot(c_q, wuq_ref[...])
    for hp in range(H_A // 2):
        ql = _dot(q[:, hp * 128:(hp + 1) * 128].astype(BF), wuk_ref[hp])
        for e in range(2):
            hh = 2 * hp + e
            qr = _rope(q[:, 512 + hh * 128:512 + (hh + 1) * 128], cos, sin, DH_ROPE // 2)
            qm_ref[hh, :, :R_KV] = ql[:, e * 128:(e + 1) * 128].astype(BF)
            qm_ref[hh, :, R_KV:] = qr.astype(BF)


def _even_project(x, g, win, gq, gkv, wuq, wuk, cos, sin):
    t, d = x.shape
    tm = min(512, t)
    nper = cos.shape[0] // tm
    row = lambda i: (i, 0)
    outs = [((H_A, t, 256), BF, pl.BlockSpec((H_A, tm, 256), lambda i: (0, i, 0))),
            ((t, 256), BF, pl.BlockSpec((tm, 256), row)),
            ((t, R_KV), F32, pl.BlockSpec((tm, R_KV), row)),
            ((t, DH_ROPE), F32, pl.BlockSpec((tm, DH_ROPE), row)),
            ((t, 512), BF, pl.BlockSpec((tm, 512), row)),
            ((t, 512), BF, pl.BlockSpec((tm, 512), row)),
            ((t, 512), BF, pl.BlockSpec((tm, 512), row)),
            ((t, 512), F32, pl.BlockSpec((tm, 512), row)),
            ((t, 512), F32, pl.BlockSpec((tm, 512), row))]
    return pl.pallas_call(
        _even_proj_body, grid=(t // tm,),
        in_specs=[pl.BlockSpec((tm, d), row), _resident((1, d)), _resident(win.shape), _resident((1, R_Q)),
                  _resident((1, R_KV)), _resident(wuq.shape), _resident(wuk.shape),
                  pl.BlockSpec((tm, LANES), lambda i: (i % nper, 0)),
                  pl.BlockSpec((tm, LANES), lambda i: (i % nper, 0))],
        out_specs=[o[2] for o in outs],
        out_shape=[jax.ShapeDtypeStruct(o[0], o[1]) for o in outs],
        compiler_params=_cp("parallel"), name="even_project")(x, g, win, gq, gkv, wuq, wuk, cos, sin)


def _mla_prompt_body(q_ref, kv_ref, o_ref, m_scr, l_scr, acc_scr):
    i = pl.program_id(1)
    rows = H_A * BLK
    q = q_ref[...].reshape(rows, 256)
    m_scr[...] = jnp.full((rows, LANES), NEG_INF, F32)
    l_scr[...] = jnp.zeros((rows, LANES), F32)
    acc_scr[...] = jnp.zeros((rows, R_KV), F32)
    tk = 2 * BLK
    c = MLA_SCALE * math.log2(math.e)

    def fold(off, causal):
        k = kv_ref[pl.ds(off, tk), :]
        s = _dot_nt(q, k)
        if causal:
            qpos = i * BLK + lax.broadcasted_iota(jnp.int32, (rows, tk), 0) % BLK
            s = jnp.where(off + lax.broadcasted_iota(jnp.int32, (rows, tk), 1) <= qpos, s, NEG_INF)
        m_prev = m_scr[...]
        m_new = jnp.maximum(m_prev, jnp.max(s, axis=1, keepdims=True))
        p = jnp.exp2((s - jnp.concatenate([m_new, m_new], axis=1)) * c)
        alpha = jnp.exp2((m_prev - m_new) * c)
        l_scr[...] = alpha * l_scr[...] + jnp.sum(p, axis=1, keepdims=True)
        acc_scr[...] = alpha * acc_scr[...] + _dot(p.astype(BF), k[:, :R_KV])
        m_scr[...] = m_new

    n_full = i // 2

    def full(j, carry):
        fold(pl.multiple_of(j * tk, tk), False)
        return carry

    lax.fori_loop(0, n_full, full, 0)
    fold(pl.multiple_of(n_full * tk, tk), True)
    o_ref[...] = (acc_scr[...] / l_scr[...]).reshape(H_A, BLK, R_KV).astype(BF)


def _mla_prompt(qm, km, b, s):
    nq = s // BLK
    rows = H_A * BLK
    return pl.pallas_call(
        _mla_prompt_body, grid=(b, nq),
        in_specs=[pl.BlockSpec((H_A, BLK, 256), lambda bb, i: (0, bb * nq + i, 0)),
                  pl.BlockSpec((s, 256), lambda bb, i: (bb, 0))],
        out_specs=pl.BlockSpec((H_A, BLK, R_KV), lambda bb, i: (0, bb * nq + i, 0)),
        out_shape=jax.ShapeDtypeStruct((H_A, b * s, R_KV), BF),
        scratch_shapes=[pltpu.VMEM((rows, LANES), F32), pltpu.VMEM((rows, LANES), F32),
                        pltpu.VMEM((rows, R_KV), F32)],
        compiler_params=_cp("parallel", "parallel"), name="mla_prompt")(qm, km)


def _sb_prompt_body(q_ref, k_ref, v_ref, o_ref, qe_scr, c_scr, acc_scr):
    i = pl.program_id(1)
    rows = H_B * BLK
    first = _head_mask((BLK, LANES), 0, DH_B)
    upper = _strict_upper(BLK)
    for p in range(H_B // 2):
        q = q_ref[:, p * LANES:(p + 1) * LANES].astype(F32)
        qe_scr[2 * p] = jnp.where(first, q, 0.0).astype(BF)
        qe_scr[2 * p + 1] = jnp.where(first, 0.0, q).astype(BF)
    c_scr[...] = jnp.zeros(c_scr.shape, F32)
    acc_scr[...] = jnp.zeros(acc_scr.shape, F32)

    def fold(j, diagonal):
        off = pl.multiple_of(j * BLK, BLK)
        z = jnp.concatenate(
            [_dot_nt(qe_scr[h], k_ref[pl.ds(off, BLK), (h // 2) * LANES:(h // 2 + 1) * LANES]) for h in range(H_B)],
            axis=0) * SB_SCALE
        ls_pos, ls_neg = _log_sigmoid_pair(z)
        if diagonal:
            valid = (lax.broadcasted_iota(jnp.int32, (rows, BLK), 1)
                     < lax.broadcasted_iota(jnp.int32, (rows, BLK), 0) % BLK)
            ls_neg = jnp.where(valid, ls_neg, 0.0)
        c = c_scr[...]
        w = jnp.exp(ls_pos + _suffix_sum(ls_neg, upper) + c)
        if diagonal:
            w = jnp.where(valid, w, 0.0)
        w = w.astype(BF)
        for p in range(H_B // 2):
            v = v_ref[pl.ds(off, BLK), p * LANES:(p + 1) * LANES]
            d0 = _dot(w[2 * p * BLK:(2 * p + 1) * BLK], v)
            d1 = _dot(w[(2 * p + 1) * BLK:(2 * p + 2) * BLK], v)
            acc_scr[:, p * LANES:(p + 1) * LANES] += jnp.where(first, d0, d1)
        c = c + jnp.sum(ls_neg, axis=1, keepdims=True)
        c_scr[...] = c
        return (jnp.max(c) > SB_DEAD).astype(jnp.int32)

    alive = fold(i, True)
    lax.while_loop(lambda st: (st[0] >= 0) & (st[1] > 0),
                   lambda st: (st[0] - 1, fold(st[0], False)), (i - 1, alive))
    o_ref[...] = acc_scr[...].astype(BF)


def _sb_prompt(qb, kb, vb, b, s):
    nq = s // BLK
    w = H_B * DH_B
    return pl.pallas_call(
        _sb_prompt_body, grid=(b, nq),
        in_specs=[pl.BlockSpec((BLK, w), lambda bb, i: (bb * nq + i, 0)),
                  pl.BlockSpec((s, w), lambda bb, i: (bb, 0)),
                  pl.BlockSpec((s, w), lambda bb, i: (bb, 0))],
        out_specs=pl.BlockSpec((BLK, w), lambda bb, i: (bb * nq + i, 0)),
        out_shape=jax.ShapeDtypeStruct((b * s, w), BF),
        scratch_shapes=[pltpu.VMEM((H_B, BLK, LANES), BF), pltpu.VMEM((H_B * BLK, 1), F32),
                        pltpu.VMEM((BLK, w), F32)],
        compiler_params=_cp("parallel", "parallel"), name="sb_prompt")(qb, kb, vb)


def _even_out_body(x_ref, ol_ref, ob_ref, wuv_ref, wout_ref, o_ref):
    parts = []
    for hp in range(H_A // 2):
        ol = jnp.concatenate([ol_ref[2 * hp], ol_ref[2 * hp + 1]], axis=1)
        parts.append(_dot(ol, wuv_ref[hp]).astype(BF))
    oa = jnp.concatenate(parts, axis=1)
    na = H_A * DH_V
    o_ref[...] = x_ref[...] + _dot(oa, wout_ref[:na, :]) + _dot(ob_ref[...], wout_ref[na:, :])


def _even_output(x, olat, ob, wuv, wout):
    t, d = x.shape
    tm = min(512, t)
    row = lambda i: (i, 0)
    return pl.pallas_call(
        _even_out_body, grid=(t // tm,),
        in_specs=[pl.BlockSpec((tm, d), row), pl.BlockSpec((H_A, tm, R_KV), lambda i: (0, i, 0)),
                  pl.BlockSpec((tm, 512), row), _resident(wuv.shape), _resident(wout.shape)],
        out_specs=pl.BlockSpec((tm, d), row),
        out_shape=jax.ShapeDtypeStruct((t, d), F32),
        compiler_params=_cp("parallel"), name="even_output")(x, olat, ob, wuv, wout)


def _mla_sample_body(pt_ref, q_ref, new_ref, *rest, group, n_pages, t_new):
    lat_hbm, kr_hbm, o_ref, lat_buf, kr_buf, sem, m_scr, l_scr, acc_scr = rest
    b = pl.program_id(0)
    rows = q_ref.shape[1]
    n_groups = n_pages // group

    def copies(grp, slot):
        out = []
        for g in range(group):
            pg = pt_ref[b, grp * group + g]
            out.append(pltpu.make_async_copy(lat_hbm.at[pg], lat_buf.at[slot, pl.ds(g * PAGE, PAGE), :],
                                             sem.at[0, slot]))
            out.append(pltpu.make_async_copy(kr_hbm.at[pg], kr_buf.at[slot, :, pl.ds(g * PAGE, PAGE)],
                                             sem.at[1, slot]))
        return out

    def start(grp, slot):
        for cp in copies(grp, slot):
            cp.start()

    def wait(grp, slot):
        for cp in copies(grp, slot):
            cp.wait()

    start(0, 0)
    m_scr[...] = jnp.full((rows, LANES), NEG_INF, F32)
    l_scr[...] = jnp.zeros((rows, LANES), F32)
    acc_scr[...] = jnp.zeros((rows, R_KV), F32)
    q = q_ref[0]
    ql = q[:, :R_KV]
    qr = q[:, R_KV:R_KV + DH_ROPE]

    def fold(grp, carry):
        slot = lax.rem(grp, 2)
        wait(grp, slot)

        @pl.when(grp + 1 < n_groups)
        def _():
            start(grp + 1, 1 - slot)

        lat = lat_buf[slot].astype(BF)
        s = (_dot_nt(ql, lat) + _dot(qr, kr_buf[slot].astype(BF))) * MLA_SCALE
        m_prev = m_scr[...]
        m_new = jnp.maximum(m_prev, jnp.max(s, axis=1, keepdims=True))
        pr = jnp.exp(s - m_new[:, :1])
        alpha = jnp.exp(m_prev - m_new)
        l_scr[...] = alpha * l_scr[...] + jnp.sum(pr, axis=1, keepdims=True)
        acc_scr[...] = alpha * acc_scr[...] + _dot(pr.astype(BF), lat)
        m_scr[...] = m_new
        return carry

    lax.fori_loop(0, n_groups, fold, 0)

    qf = q.astype(F32)
    tq = lax.broadcasted_iota(jnp.int32, (rows, 1), 0) // H_A
    m = m_scr[...]
    l = l_scr[...]
    acc = acc_scr[...]
    for j in range(t_new):
        kn = new_ref[0, j:j + 1, :].astype(F32)
        s = jnp.sum(qf * kn, axis=1, keepdims=True) * MLA_SCALE
        s = jnp.where(j <= tq, s, NEG_INF)
        m_new = jnp.maximum(m, s)
        pj = jnp.exp(s - m_new)
        alpha = jnp.exp(m - m_new)
        l = alpha * l + pj
        acc = alpha * acc + pj.astype(BF).astype(F32) * kn[:, :R_KV]
        m = m_new
    o_ref[0] = (acc / l).astype(BF)


def _mla_sample(page_table, q, new, lat_pool, kr_pool, group):
    bd, n_pages = page_table.shape
    rows = q.shape[1]
    t_new = new.shape[1]
    own = lambda b, pt: (b, 0, 0)
    grid_spec = pltpu.PrefetchScalarGridSpec(
        num_scalar_prefetch=1, grid=(bd,),
        in_specs=[pl.BlockSpec((1, rows, 256), own), pl.BlockSpec((1, t_new, 256), own),
                  pl.BlockSpec(memory_space=pl.ANY), pl.BlockSpec(memory_space=pl.ANY)],
        out_specs=pl.BlockSpec((1, rows, R_KV), own),
        scratch_shapes=[pltpu.VMEM((2, group * PAGE, R_KV), F32), pltpu.VMEM((2, DH_ROPE, group * PAGE), F32),
                        pltpu.SemaphoreType.DMA((2, 2)),
                        pltpu.VMEM((rows, LANES), F32), pltpu.VMEM((rows, LANES), F32),
                        pltpu.VMEM((rows, R_KV), F32)])
    return pl.pallas_call(
        functools.partial(_mla_sample_body, group=group, n_pages=n_pages, t_new=t_new), grid_spec=grid_spec,
        out_shape=jax.ShapeDtypeStruct((bd, rows, R_KV), BF),
        compiler_params=_cp("arbitrary"), name="mla_sample")(page_table, q, new, lat_pool, kr_pool)


def _sb_sample_body(pt_ref, q_ref, kn_ref, vn_ref, k_hbm, v_hbm, o_ref, kbuf, vbuf, sem, c_scr, acc_scr,
                    *, layer, n_pages, t_new):
    b = pl.program_id(0)
    rows = H_B * 8
    slot_t = lax.broadcasted_iota(jnp.int32, (rows, 1), 0) % 8
    real = slot_t < t_new

    def copies(p, slot):
        pg = pt_ref[b, p]
        return (pltpu.make_async_copy(k_hbm.at[layer, pg], kbuf.at[slot], sem.at[0, slot]),
                pltpu.make_async_copy(v_hbm.at[layer, pg], vbuf.at[slot], sem.at[1, slot]))

    def start(p, slot):
        for cp in copies(p, slot):
            cp.start()

    def wait(p, slot):
        for cp in copies(p, slot):
            cp.wait()

    start(n_pages - 1, 0)

    q = q_ref[0]
    qf = q.astype(BF).astype(F32)
    q3 = q.reshape(H_B, 8, DH_B).astype(BF)
    c = jnp.zeros((rows, 1), F32)
    acc = jnp.zeros((rows, DH_B), F32)
    for j in reversed(range(t_new)):
        kn = kn_ref[0, j].astype(BF).astype(F32)
        vn = vn_ref[0, j].astype(BF).astype(F32)
        z = jnp.sum(qf * kn, axis=1, keepdims=True) * SB_SCALE
        ls_pos, ls_neg = _log_sigmoid_pair(z)
        valid = j < slot_t
        w = jnp.where(valid, jnp.exp(ls_pos + c), 0.0)
        acc = acc + w.astype(BF).astype(F32) * vn
        c = c + jnp.where(valid, ls_neg, 0.0)
    c_scr[...] = c
    acc_scr[...] = acc
    upper = _strict_upper(PAGE)

    def alive_of(cc):
        return (jnp.max(jnp.where(real, cc, NEG_INF)) > SB_DEAD).astype(jnp.int32)

    def body(st):
        p, _ = st
        slot = lax.rem(n_pages - 1 - p, 2)
        wait(p, slot)

        @pl.when(p >= 1)
        def _():
            start(p - 1, 1 - slot)

        kt = kbuf[slot].astype(BF)
        vt = vbuf[slot].astype(BF)
        z = lax.dot_general(q3, kt, (((2,), (1,)), ((0,), (0,))), preferred_element_type=F32)
        z = z.reshape(rows, PAGE) * SB_SCALE
        ls_pos, ls_neg = _log_sigmoid_pair(z)
        cc = c_scr[...]
        w = jnp.exp(ls_pos + _suffix_sum(ls_neg, upper) + cc).reshape(H_B, 8, PAGE).astype(BF)
        d = lax.dot_general(w, vt, (((2,), (2,)), ((0,), (0,))), preferred_element_type=F32)
        acc_scr[...] = acc_scr[...] + d.reshape(rows, DH_B)
        cc = cc + jnp.sum(ls_neg, axis=1, keepdims=True)
        c_scr[...] = cc
        return p - 1, alive_of(cc)

    p_next, _ = lax.while_loop(lambda st: (st[0] >= 0) & (st[1] > 0), body, (n_pages - 1, alive_of(c)))

    @pl.when(p_next >= 0)
    def _():
        wait(p_next, lax.rem(n_pages - 1 - p_next, 2))

    o_ref[0] = acc_scr[...]


def _sb_sample(page_table, q, k_new, v_new, k_pool, v_pool, layer):
    bd, n_pages = page_table.shape
    t_new = q.shape[1]
    rows = H_B * 8
    q_rows = jnp.pad(jnp.transpose(q, (0, 2, 1, 3)), ((0, 0), (0, 0), (0, 8 - t_new), (0, 0))).reshape(bd, rows, DH_B)
    spread = lambda a: jnp.broadcast_to(a[:, :, :, None, :], (bd, t_new, H_B, 8, DH_B)).reshape(bd, t_new, rows, DH_B)
    own3 = lambda b, pt: (b, 0, 0)
    own4 = lambda b, pt: (b, 0, 0, 0)
    grid_spec = pltpu.PrefetchScalarGridSpec(
        num_scalar_prefetch=1, grid=(bd,),
        in_specs=[pl.BlockSpec((1, rows, DH_B), own3), pl.BlockSpec((1, t_new, rows, DH_B), own4),
                  pl.BlockSpec((1, t_new, rows, DH_B), own4),
                  pl.BlockSpec(memory_space=pl.ANY), pl.BlockSpec(memory_space=pl.ANY)],
        out_specs=pl.BlockSpec((1, rows, DH_B), own3),
        scratch_shapes=[pltpu.VMEM((2, H_B, DH_B, PAGE), F32), pltpu.VMEM((2, H_B, DH_B, PAGE), F32),
                        pltpu.SemaphoreType.DMA((2, 2)), pltpu.VMEM((rows, 1), F32), pltpu.VMEM((rows, DH_B), F32)])
    by_head = lambda pool: jnp.transpose(pool, (0, 1, 3, 4, 2))
    out = pl.pallas_call(
        functools.partial(_sb_sample_body, layer=layer, n_pages=n_pages, t_new=t_new), grid_spec=grid_spec,
        out_shape=jax.ShapeDtypeStruct((bd, rows, DH_B), F32),
        compiler_params=_cp("arbitrary"), name="sb_sample")(
            page_table, q_rows, spread(k_new), spread(v_new), by_head(k_pool), by_head(v_pool))
    return jnp.transpose(out.reshape(bd, H_B, 8, DH_B)[:, :, :t_new], (0, 2, 1, 3)).reshape(bd * t_new, H_B * DH_B)


def _odd_proj_body(x_ref, g_ref, win_ref, cos_ref, sin_ref, z_ref):
    h = _rms(x_ref[...], g_ref[...]).astype(BF)
    z = _dot(h, win_ref[...])
    cos, sin = cos_ref[...], sin_ref[...]
    w = H_C * DH_C
    for c in range(z.shape[1] // w):
        part = z[:, c * w:(c + 1) * w]
        if c % 3 != 2:
            part = _rope(part, cos, sin, DH_C // 2)
        z_ref[:, c * w:(c + 1) * w] = part


def _odd_project(x, g, win, cos, sin):
    t, d = x.shape
    n = win.shape[1]
    tm = min(512, t)
    nper = cos.shape[0] // tm
    w = H_C * DH_C
    return pl.pallas_call(
        _odd_proj_body, grid=(t // tm,),
        in_specs=[pl.BlockSpec((tm, d), lambda i: (i, 0)), _resident((1, d)), _resident(win.shape),
                  pl.BlockSpec((tm, w), lambda i: (i % nper, 0)), pl.BlockSpec((tm, w), lambda i: (i % nper, 0))],
        out_specs=pl.BlockSpec((tm, n), lambda i: (i, 0)),
        out_shape=jax.ShapeDtypeStruct((t, n), F32),
        compiler_params=_cp("parallel"), name="odd_project")(x, g, win, cos, sin)


def _dil_prompt_body(q0_ref, q1_ref, k0_ref, k1_ref, v0_ref, v1_ref, o_ref, lse_ref, *, dil, nback):
    s_len = q0_ref.shape[0]
    w = H_C * DH_C
    win = BLK * dil
    both = lambda r0, r1, rows: jnp.concatenate([r0[rows, :], r1[rows, :]], axis=1)
    row = lax.broadcasted_iota(jnp.int32, (H_C * BLK, 2 * BLK), 0) % BLK
    col = lax.broadcasted_iota(jnp.int32, (H_C * BLK, 2 * BLK), 1)
    diff = row + BLK - col
    band = (diff >= 0) & (diff <= nback)
    masks = [_head_mask((BLK, w), h, DH_C) for h in range(H_C)]

    def block(jb, carry):
        base = pl.multiple_of(jb * win, win)
        prev = pl.multiple_of(jnp.maximum(jb - 1, 0) * win, win)
        first_col = jnp.where(jb > 0, 0, BLK)
        mask = band & (col >= first_col)
        for r in range(dil):
            cur = pl.ds(base + r, BLK, stride=dil)
            old = pl.ds(prev + r, BLK, stride=dil)
            q = both(q0_ref, q1_ref, cur)
            kk = jnp.concatenate([both(k0_ref, k1_ref, old), both(k0_ref, k1_ref, cur)], axis=0).astype(BF)
            vv = jnp.concatenate([both(v0_ref, v1_ref, old), both(v0_ref, v1_ref, cur)], axis=0).astype(BF)
            s = jnp.concatenate([_dot_nt(jnp.where(masks[h], q, 0.0).astype(BF), kk) for h in range(H_C)], axis=0)
            s = jnp.where(mask, s * DIL_SCALE, NEG_INF)
            m = jnp.max(s, axis=1, keepdims=True)
            p = jnp.exp(s - m)
            l = jnp.sum(p, axis=1, keepdims=True)
            pb = p.astype(BF)
            lse_rows = m + jnp.log(l)
            o = jnp.zeros((BLK, w), F32)
            lse = jnp.zeros((BLK, w), F32)
            for h in range(H_C):
                rows_h = slice(h * BLK, (h + 1) * BLK)
                o = jnp.where(masks[h], _dot(pb[rows_h], vv) / l[rows_h], o)
                lse = jnp.where(masks[h], lse_rows[rows_h], lse)
            for half in range(2):
                o_ref[half, cur, :] = o[:, half * LANES:(half + 1) * LANES]
                lse_ref[half, cur, :] = lse[:, half * LANES:(half + 1) * LANES]
        return carry

    lax.fori_loop(0, s_len // win, block, 0)


def _dil_prompt(zr, b, s, g, dil, nback):
    spec = lambda part, half: pl.BlockSpec((s, LANES), lambda bb: (bb, 2 * (3 * g + part) + half))
    out_spec = pl.BlockSpec((2, s, LANES), lambda bb: (0, bb, 0))
    return pl.pallas_call(
        functools.partial(_dil_prompt_body, dil=dil, nback=nback), grid=(b,),
        in_specs=[spec(part, half) for part in range(3) for half in range(2)],
        out_specs=[out_spec, out_spec],
        out_shape=[jax.ShapeDtypeStruct((2, b * s, LANES), F32)] * 2,
        compiler_params=_cp("parallel"), name=f"dil_prompt_{g}")(*([zr] * 6))


def _dil_sample_body(q_ref, z_ref, b0_ref, b1_ref, b2_ref, n0_ref, n1_ref, n2_ref,
                     o_ref, lse_ref, u0_ref, u1_ref, u2_ref, *, t_new):
    w = H_C * DH_C
    rows = H_C * 8
    bufs = (b0_ref, b1_ref, b2_ref)
    news = (n0_ref, n1_ref, n2_ref)
    upds = (u0_ref, u1_ref, u2_ref)
    slot = lax.broadcasted_iota(jnp.int32, (rows, 1), 0) % 8
    pick = (lax.broadcasted_iota(jnp.int32, (rows, w), 0) // 8
            == lax.broadcasted_iota(jnp.int32, (rows, w), 1) // DH_C)
    fold = lambda a: functools.reduce(lambda x, y: x + y, [a[h * 8:(h + 1) * 8] for h in range(H_C)])
    for g, (window, dil) in enumerate(DIL_GROUPS):
        q = q_ref[0, g * rows:(g + 1) * rows, :].astype(BF)
        qf = q.astype(F32)
        knew = z_ref[0, :, (3 * g + 1) * w:(3 * g + 2) * w].astype(BF).astype(F32)
        vnew = z_ref[0, :, (3 * g + 2) * w:(3 * g + 3) * w].astype(BF).astype(F32)
        kt = bufs[g][0, 0].reshape(w, window).astype(BF)
        vt = bufs[g][0, 1].reshape(w, window).astype(BF)
        col = lax.broadcasted_iota(jnp.int32, (rows, window), 1)
        valid = (col >= slot) & (col % dil == slot % dil)
        s_buf = jnp.where(valid, _dot(q, kt) * DIL_SCALE, NEG_INF)
        s_new = []
        for j in range(t_new):
            sj = jnp.sum(qf * knew[j:j + 1, :], axis=1, keepdims=True) * DIL_SCALE
            s_new.append(jnp.where((j <= slot) & ((slot - j) % dil == 0), sj, NEG_INF))
        m = functools.reduce(jnp.maximum, s_new, jnp.max(s_buf, axis=1, keepdims=True))
        p_buf = jnp.exp(s_buf - m)
        l = jnp.sum(p_buf, axis=1, keepdims=True)
        o = _dot_nt(p_buf.astype(BF), vt)
        for j, sj in enumerate(s_new):
            pj = jnp.exp(sj - m)
            l = l + pj
            o = o + pj.astype(BF).astype(F32) * vnew[j:j + 1, :]
        o = jnp.where(pick, o / l, 0.0)
        lse = jnp.where(pick, m + jnp.log(l), 0.0)
        o_ref[0, :, g * w:(g + 1) * w] = fold(o)[:t_new]
        lse_ref[0, :, g * w:(g + 1) * w] = fold(lse)[:t_new]
        x = bufs[g][0].reshape(2 * w, window)
        moved = pltpu.roll(x, window - t_new, 1)
        tail = jnp.where(lax.broadcasted_iota(jnp.int32, (2 * w, LANES), 1) >= LANES - t_new,
                         news[g][0], moved[:, window - LANES:])
        if window > LANES:
            tail = jnp.concatenate([moved[:, :window - LANES], tail], axis=1)
        upds[g][0] = tail.reshape(2, H_C, DH_C, window)


def _dil_sample(q_rows, zr, bufs):
    bd, t_new, n = zr.shape
    w = H_C * DH_C
    own = lambda b: (b, 0, 0)
    views = [jnp.transpose(buf, (0, 2, 3, 4, 1)) for buf in bufs]
    buf_specs = [pl.BlockSpec((1,) + v.shape[1:], lambda b: (b, 0, 0, 0, 0)) for v in views]
    ng = len(DIL_GROUPS)
    news = [jnp.pad(jnp.swapaxes(zr[:, :, (3 * g + 1) * w:(3 * g + 3) * w], 1, 2),
                    ((0, 0), (0, 0), (LANES - t_new, 0))) for g in range(ng)]
    new_spec = pl.BlockSpec((1, 2 * w, LANES), own)
    outs = pl.pallas_call(
        functools.partial(_dil_sample_body, t_new=t_new), grid=(bd,),
        in_specs=[pl.BlockSpec((1, q_rows.shape[1], w), own), pl.BlockSpec((1, t_new, n), own)] + buf_specs
                 + [new_spec] * ng,
        out_specs=[pl.BlockSpec((1, t_new, ng * w), own)] * 2 + buf_specs,
        out_shape=[jax.ShapeDtypeStruct((bd, t_new, ng * w), F32)] * 2
                  + [jax.ShapeDtypeStruct(v.shape, F32) for v in views],
        compiler_params=_cp("parallel"), name="dil_sample")(q_rows, zr, *views, *news)
    return outs[0], outs[1], [jnp.transpose(u, (0, 4, 1, 2, 3)) for u in outs[2:]]


def _odd_out_body(x_ref, *rest):
    ng = len(DIL_GROUPS)
    o_refs, l_refs = rest[:ng], rest[ng:2 * ng]
    w_ref, out_ref = rest[2 * ng:]
    whole = lambda r: jnp.concatenate([r[0], r[1]], axis=1)
    ls = [whole(r) for r in l_refs]
    m = functools.reduce(jnp.maximum, ls)
    es = [jnp.exp(l - m) for l in ls]
    den = functools.reduce(lambda a, b: a + b, es)
    mix = functools.reduce(lambda a, b: a + b, [e / den * whole(r) for e, r in zip(es, o_refs)])
    out_ref[...] = x_ref[...] + _dot(mix.astype(BF), w_ref[...])


def _odd_output(x, outs, lses, wout):
    t, d = x.shape
    tm = min(512, t)
    row = lambda i: (i, 0)
    halves = pl.BlockSpec((2, tm, LANES), lambda i: (0, i, 0))
    return pl.pallas_call(
        _odd_out_body, grid=(t // tm,),
        in_specs=[pl.BlockSpec((tm, d), row)] + [halves] * (2 * len(outs)) + [_resident(wout.shape)],
        out_specs=pl.BlockSpec((tm, d), row),
        out_shape=jax.ShapeDtypeStruct((t, d), F32),
        compiler_params=_cp("parallel"), name="odd_output")(x, *outs, *lses, wout)


def _mem_attn_body(x_ref, g_ref, wq_ref, k_ref, v_ref, wo_ref, o_ref):
    x = x_ref[0]
    w = H_M * DH_M
    q = _dot(_rms(x, g_ref[...]).astype(BF), wq_ref[...])
    kk = k_ref[0].astype(BF)
    vv = v_ref[0].astype(BF)
    o = jnp.zeros((x.shape[0], w), F32)
    for h in range(H_M):
        hm = _head_mask(q.shape, h, DH_M)
        s = _dot_nt(jnp.where(hm, q, 0.0).astype(BF), kk) * MEM_SCALE
        m = jnp.max(s, axis=1, keepdims=True)
        p = jnp.exp(s - m)
        l = jnp.sum(p, axis=1, keepdims=True)
        o = jnp.where(hm, _dot(p.astype(BF), vv) / l, o)
    o_ref[0] = x + _dot(o.astype(BF), wo_ref[...])


def _mem_attend(x, g, wq, k, v, wo):
    b, s, d = x.shape
    n_mem, w = k.shape[1:]
    tm = min(512, s)
    return pl.pallas_call(
        _mem_attn_body, grid=(b, s // tm),
        in_specs=[pl.BlockSpec((1, tm, d), lambda bb, i: (bb, i, 0)), _resident((1, d)), _resident(wq.shape),
                  pl.BlockSpec((1, n_mem, w), lambda bb, i: (bb, 0, 0)),
                  pl.BlockSpec((1, n_mem, w), lambda bb, i: (bb, 0, 0)), _resident(wo.shape)],
        out_specs=pl.BlockSpec((1, tm, d), lambda bb, i: (bb, i, 0)),
        out_shape=jax.ShapeDtypeStruct((b, s, d), F32),
        compiler_params=_cp("parallel", "parallel"), name="mem_attend")(x, g, wq, k, v, wo)


def _rope_tables(pos, half, width):
    inv_freq = ROPE_THETA ** (-jnp.arange(half, dtype=F32) / half)
    ang = pos.astype(F32)[:, None] * inv_freq[None, :]
    cos, sin = jnp.cos(ang), jnp.sin(ang)
    reps = width // (2 * half)
    return (jnp.tile(jnp.concatenate([cos, cos], axis=1), (1, reps)),
            jnp.tile(jnp.concatenate([-sin, sin], axis=1), (1, reps)))


def _even_weights(w_in, w_uq, w_uk, w_uv):
    d = w_in.shape[0]
    c0 = R_Q + R_KV
    win = jnp.concatenate([w_in[:, :c0], w_in[:, c0:c0 + DH_ROPE], jnp.zeros((d, LANES - DH_ROPE), F32),
                           w_in[:, c0 + DH_ROPE:]], axis=1).astype(BF)
    nope = w_uq[:, :, :DH_NOPE].reshape(R_Q, H_A * DH_NOPE)
    rope = jnp.pad(w_uq[:, :, DH_NOPE:], ((0, 0), (0, 0), (0, LANES - DH_ROPE))).reshape(R_Q, H_A * LANES)
    wuq = jnp.concatenate([nope, rope], axis=1).astype(BF)
    uk = jnp.transpose(w_uk, (1, 2, 0)).reshape(H_A // 2, 2, DH_NOPE, R_KV)
    z = jnp.zeros_like(uk[:, 0])
    wuk = jnp.concatenate([jnp.concatenate([uk[:, 0], z], axis=2),
                           jnp.concatenate([z, uk[:, 1]], axis=2)], axis=1).astype(BF)
    uv = jnp.transpose(w_uv, (1, 0, 2)).reshape(H_A // 2, 2, R_KV, DH_V)
    zv = jnp.zeros_like(uv[:, 0])
    wuv = jnp.concatenate([jnp.concatenate([uv[:, 0], zv], axis=2),
                           jnp.concatenate([zv, uv[:, 1]], axis=2)], axis=1).astype(BF)
    return win, wuq, wuk, wuv


def kernel(x_prompt, x_sample, mem_prompt, cache_mla_latent, cache_mla_krope, cache_sb_k, cache_sb_v,
           cache_dil_w128, cache_dil_w512, cache_dil_w2048, cache_mem_k, cache_mem_v, page_table,
           ffn_norm, ffn_w_gu, ffn_w_down, mix_norm,
           ev_w_in, ev_g_q, ev_g_kv, ev_w_uq, ev_w_uk, ev_w_uv, ev_w_out,
           od_w_in, od_w_out,
           xa_norm, xa_mem_norm, xa_w_q, xa_w_kv, xa_w_o, final_norm):
    b, s, d = x_prompt.shape
    bd, t_new, _ = x_sample.shape
    n_mem = mem_prompt.shape[1]
    depth = mix_norm.shape[0]
    n_pages = page_table.shape[1]
    past = n_pages * PAGE
    pos_p = jnp.arange(s, dtype=jnp.int32)
    pos_s = jnp.tile(past + jnp.arange(t_new, dtype=jnp.int32), bd)
    dil_bufs = (cache_dil_w128, cache_dil_w512, cache_dil_w2048)
    w_hc = H_C * DH_C

    xp = x_prompt.reshape(b * s, d)
    xs = x_sample.reshape(bd * t_new, d)
    mem = mem_prompt.reshape(b * n_mem, d)
    row = lambda v: v.reshape(1, -1)

    p_lat, p_kpe, p_sbk, p_sbv, s_lat, s_kpe, s_sbk, s_sbv = ([] for _ in range(8))
    p_dil = [[] for _ in DIL_GROUPS]
    s_dil = [[] for _ in DIL_GROUPS]
    p_memk, p_memv = [], []

    for layer in range(depth):
        i = layer // 2
        wgu, wd = ffn_w_gu[layer, 0].astype(BF), ffn_w_down[layer, 0].astype(BF)
        xp = _ffn(xp, row(ffn_norm[layer, 0]), wgu, wd, row(final_norm), final=False)
        xs = _ffn(xs, row(ffn_norm[layer, 0]), wgu, wd, row(final_norm), final=False)
        if layer % 2 == 0:
            win, wuq, wuk, wuv = _even_weights(ev_w_in[i], ev_w_uq[i], ev_w_uk[i], ev_w_uv[i])
            wout = ev_w_out[i].astype(BF)
            args = (row(mix_norm[layer]), win, row(ev_g_q[i]), row(ev_g_kv[i]), wuq, wuk)
            cos, sin = _rope_tables(pos_p, DH_ROPE // 2, LANES)
            qm, km, lat, kpe, qb, kbb, vbb, kb, vb = _even_project(xp, *args, cos, sin)
            olat = _mla_prompt(qm, km, b, s)
            ob = _sb_prompt(qb, kbb, vbb, b, s)
            xp = _even_output(xp, olat, ob, wuv, wout)
            p_lat.append(lat.reshape(b, s, R_KV))
            p_kpe.append(kpe.reshape(b, s, DH_ROPE))
            p_sbk.append(kb.reshape(b, s, H_B, DH_B))
            p_sbv.append(vb.reshape(b, s, H_B, DH_B))
            cos, sin = _rope_tables(pos_s, DH_ROPE // 2, LANES)
            qm, km, lat, kpe, qb, kbb, vbb, kb, vb = _even_project(xs, *args, cos, sin)
            q_rows = jnp.transpose(qm.reshape(H_A, bd, t_new, 256), (1, 2, 0, 3)).reshape(bd, t_new * H_A, 256)
            olat = _mla_sample(page_table, q_rows, km.astype(F32).reshape(bd, t_new, 256),
                               cache_mla_latent[i], jnp.swapaxes(cache_mla_krope[i], 1, 2),
                               group=min(16, n_pages))
            olat = jnp.transpose(olat.reshape(bd * t_new, H_A, R_KV), (1, 0, 2))
            heads = lambda a: a.astype(F32).reshape(bd, t_new, H_B, DH_B)
            ob = _sb_sample(page_table, heads(qb), heads(kb), heads(vb), cache_sb_k, cache_sb_v, layer=i)
            xs = _even_output(xs, olat, ob.astype(BF), wuv, wout)
            s_lat.append(lat.reshape(bd, t_new, R_KV))
            s_kpe.append(kpe.reshape(bd, t_new, DH_ROPE))
            s_sbk.append(kb.reshape(bd, t_new, H_B, DH_B))
            s_sbv.append(vb.reshape(bd, t_new, H_B, DH_B))
        else:
            win = od_w_in[i].astype(BF)
            wout = od_w_out[i].astype(BF)
            cos, sin = _rope_tables(pos_p, DH_C // 2, w_hc)
            zp = _odd_project(xp, row(mix_norm[layer]), win, cos, sin)
            cos, sin = _rope_tables(pos_s, DH_C // 2, w_hc)
            zs = _odd_project(xs, row(mix_norm[layer]), win, cos, sin)
            outs, lses = [], []
            zp3 = zp.reshape(b, s, -1)
            zs3 = zs.reshape(bd, t_new, -1)
            bufs = []
            for g, (window, dil) in enumerate(DIL_GROUPS):
                o, lse = _dil_prompt(zp, b, s, g, dil, window // dil)
                outs.append(o)
                lses.append(lse)
                kv = zp3[:, s - min(window, s):, (3 * g + 1) * w_hc:(3 * g + 3) * w_hc]
                p_dil[g].append(kv.reshape(b, -1, 2, H_C, DH_C))
                buf = dil_bufs[g][i]
                assert buf.shape[1] == window and t_new <= 8
                bufs.append(buf)
            xp = _odd_output(xp, outs, lses, wout)
            ng = len(DIL_GROUPS)
            q_all = jnp.stack([zs3[:, :, 3 * g * w_hc:(3 * g + 1) * w_hc] for g in range(ng)], axis=1)
            q_all = jnp.pad(q_all, ((0, 0), (0, 0), (0, 8 - t_new), (0, 0)))
            q_rows = jnp.where(
                (jnp.arange(w_hc) // DH_C)[None, None, :] == jnp.arange(H_C)[:, None, None],
                q_all[:, :, None, :, :], 0.0).reshape(bd, ng * H_C * 8, w_hc)
            o_s, lse_s, new_bufs = _dil_sample(q_rows, zs3, bufs)
            for g in range(ng):
                s_dil[g].append(new_bufs[g])
            halves = lambda a: jnp.transpose(a.reshape(bd * t_new, ng, 2, LANES), (1, 2, 0, 3))
            o_s, lse_s = halves(o_s), halves(lse_s)
            xs = _odd_output(xs, [o_s[g] for g in range(ng)], [lse_s[g] for g in range(ng)], wout)
        mk, mv = _mem_kv(mem, row(xa_mem_norm[layer]), xa_w_kv[layer].astype(BF))
        p_memk.append(mk.reshape(b, n_mem, H_M, DH_M))
        p_memv.append(mv.reshape(b, n_mem, H_M, DH_M))
        wq, wo = xa_w_q[layer].astype(BF), xa_w_o[layer].astype(BF)
        xp = _mem_attend(xp.reshape(b, s, d), row(xa_norm[layer]), wq, mk.reshape(b, n_mem, -1),
                         mv.reshape(b, n_mem, -1), wo).reshape(b * s, d)
        xs = _mem_attend(xs.reshape(bd, t_new, d), row(xa_norm[layer]), wq,
                         cache_mem_k[layer].reshape(bd, n_mem, -1), cache_mem_v[layer].reshape(bd, n_mem, -1),
                         wo).reshape(bd * t_new, d)
        wgu, wd = ffn_w_gu[layer, 1].astype(BF), ffn_w_down[layer, 1].astype(BF)
        last = layer == depth - 1
        xp = _ffn(xp, row(ffn_norm[layer, 1]), wgu, wd, row(final_norm), final=last)
        xs = _ffn(xs, row(ffn_norm[layer, 1]), wgu, wd, row(final_norm), final=last)

    return (xp.reshape(b, s, d), xs.reshape(bd, t_new, d),
            jnp.stack(p_lat), jnp.stack(p_kpe), jnp.stack(p_sbk), jnp.stack(p_sbv),
            jnp.stack(p_dil[0]), jnp.stack(p_dil[1]), jnp.stack(p_dil[2]),
            jnp.stack(p_memk), jnp.stack(p_memv),
            jnp.stack(s_lat), jnp.stack(s_kpe), jnp.stack(s_sbk), jnp.stack(s_sbv),
            jnp.stack(s_dil[0]), jnp.stack(s_dil[1]), jnp.stack(s_dil[2]))
```
